```python
import jax, jax.numpy as jnp
from jax import lax
import numpy as np

D_MODEL = 1024
BATCH = 2
SEQ = 8192
DEPTH = 1
DEC_BATCH = 32
DEC_SEQ = 16
PAST_LEN = 2048

CHUNK = 64
N_META = 16
NORM_EPS = 1e-6
RW_HEADS = 16
RW_HEAD_DIM = 64
RW_WIDTH = RW_HEADS * RW_HEAD_DIM
RW_DECAY_RANK = 64
RW_A_RANK = 64
RW_SHIFT_WIDTH = 3 * RW_WIDTH + RW_DECAY_RANK + RW_A_RANK
RW_GN_EPS = 64e-5
GD_HEADS = 8
GD_HEAD_DIM = 128
GD_WIDTH = GD_HEADS * GD_HEAD_DIM
GD_CONV = 4
GD_CONV_WIDTH = 3 * GD_WIDTH
N_BRANCH = 2
OFF_RW_SHIFT = 0
OFF_RW_GATE = OFF_RW_SHIFT + RW_SHIFT_WIDTH
OFF_GD_CONV = OFF_RW_GATE + RW_WIDTH
OFF_GD_BETA = OFF_GD_CONV + GD_CONV_WIDTH
OFF_GD_ALPHA = OFF_GD_BETA + GD_HEADS
OFF_GD_GATE = OFF_GD_ALPHA + GD_HEADS
OFF_MERGE = OFF_GD_GATE + GD_WIDTH
PROJ_WIDTH = OFF_MERGE + N_BRANCH * D_MODEL

kernel_name = 'rwkv7_gdn_gated_merge_stream_step'


def rmsnorm(x, gain, eps=NORM_EPS):
    xf = x.astype(jnp.float32)
    y = xf * lax.rsqrt(jnp.mean(xf * xf, axis=-1, keepdims=True) + eps)
    return (y * gain.astype(jnp.float32)).astype(x.dtype)


def l2norm(x, eps=1e-6):
    return x * lax.rsqrt(jnp.sum(x * x, axis=-1, keepdims=True) + eps)


def _heads(t, n_heads):
    return t.reshape(t.shape[:-1] + (n_heads, -1))


def rwkv7_branch(xs, gate, S0, p):
    B, T, _ = xs.shape
    f32 = jnp.float32
    xs = xs.astype(f32)
    r, k, v, wl, al = jnp.split(xs, [RW_WIDTH, 2 * RW_WIDTH, 3 * RW_WIDTH, 3 * RW_WIDTH + RW_DECAY_RANK], axis=-1)
    w_log = -jax.nn.softplus(-(p['rw_w0'] + jnp.tanh(wl) @ p['rw_w2'])) - 0.5
    decay = jnp.exp(-jnp.exp(w_log))
    a = jax.nn.sigmoid(p['rw_a0'] + al @ p['rw_a2'])
    kk = l2norm(_heads(k * p['rw_k_k'], RW_HEADS))
    k = k * (1.0 + (a - 1.0) * p['rw_k_a'])
    r, k, v, decay, a = (_heads(t, RW_HEADS) for t in (r, k, v, decay, a))

    def step(S, inp):
        r_t, w_t, k_t, v_t, a_t, b_t = inp
        sa = jnp.einsum('bhvk,bhk->bhv', S, a_t)
        S = S * w_t[:, :, None, :] + sa[..., None] * b_t[:, :, None, :] + v_t[..., None] * k_t[:, :, None, :]
        return S, jnp.einsum('bhvk,bhk->bhv', S, r_t)

    time_major = lambda t: jnp.moveaxis(t, 1, 0)
    S, y = lax.scan(step, S0.astype(f32), tuple(time_major(t) for t in (r, decay, k, v, -kk, kk * a)))
    y = jnp.moveaxis(y, 0, 1)
    mean = jnp.mean(y, axis=-1, keepdims=True)
    var = jnp.mean(jnp.square(y - mean), axis=-1, keepdims=True)
    y = ((y - mean) * lax.rsqrt(var + RW_GN_EPS)).reshape(B, T, RW_WIDTH) * p['rw_ln_w'] + p['rw_ln_b']
    bonus = jnp.sum(r * k * p['rw_r_k'], axis=-1, keepdims=True) * v
    y = (y + bonus.reshape(B, T, RW_WIDTH)) * jax.nn.silu(gate.astype(f32))
    return y, S


def gdn_chunk(S, inp):
    q, k, v, g, beta = inp
    L = q.shape[1]
    q, k, v = (jnp.swapaxes(t, 1, 2) for t in (q, k, v))
    g, beta = jnp.swapaxes(g, 1, 2), jnp.swapaxes(beta, 1, 2)
    G = jnp.cumsum(g, axis=-1)
    idx = jnp.arange(L)
    incl = idx[:, None] >= idx[None, :]
    strict = idx[:, None] > idx[None, :]
    decay = jnp.exp(jnp.where(incl, G[..., :, None] - G[..., None, :], -jnp.inf))
    kk = jnp.einsum('bhik,bhjk->bhij', k, k)
    A = jnp.eye(L, dtype=jnp.float32) + jnp.where(strict, beta[..., :, None] * decay * kk, 0.0)
    rhs = jnp.concatenate([(beta * jnp.exp(G))[..., None] * k, beta[..., None] * v], axis=-1)
    sol = lax.linalg.triangular_solve(A, rhs, left_side=True, lower=True, unit_diagonal=True)
    W, U = sol[..., :GD_HEAD_DIM], sol[..., GD_HEAD_DIM:]
    delta = U - jnp.einsum('bhik,bhkv->bhiv', W, S)
    qk = jnp.einsum('bhik,bhjk->bhij', q, k) * decay
    o = jnp.exp(G)[..., None] * jnp.einsum('bhik,bhkv->bhiv', q, S) + jnp.einsum('bhij,bhjv->bhiv', qk, delta)
    G_last = G[..., -1:]
    S_new = jnp.exp(G_last)[..., None] * S + jnp.einsum('bhjk,bhjv->bhkv', k * jnp.exp(G_last - G)[..., None], delta)
    return S_new, jnp.swapaxes(o, 1, 2)


def gdn_sequence(S, q, k, v, g, beta, n_lead):
    B = q.shape[0]
    outs = []
    if n_lead:
        S, o = gdn_chunk(S, tuple(t[:, :n_lead] for t in (q, k, v, g, beta)))
        outs.append(o)
    rest = [t[:, n_lead:] for t in (q, k, v, g, beta)]
    T = rest[0].shape[1]
    L = min(CHUNK, T)
    n = T // L
    to_blocks = lambda t: jnp.moveaxis(t.reshape((B, n, L) + t.shape[2:]), 1, 0)
    S, o = lax.scan(gdn_chunk, S, tuple(to_blocks(t) for t in rest))
    outs.append(jnp.moveaxis(o, 0, 1).reshape((B, T) + o.shape[3:]))
    return jnp.concatenate(outs, axis=1), S


def gdn_branch(conv_in, conv_prev, beta_logit, alpha_logit, gate, S0, n_lead, p):
    B, T, _ = conv_in.shape
    f32 = jnp.float32
    xp = jnp.concatenate([conv_prev.astype(conv_in.dtype), conv_in], axis=1)
    w = p['gd_conv_w']
    conv = sum(w[i] * xp[:, i:i + T] for i in range(GD_CONV))
    conv_new = xp[:, T:]
    act = jax.nn.silu(conv.astype(f32))
    q, k, v = (_heads(t, GD_HEADS) for t in jnp.split(act, 3, axis=-1))
    q = l2norm(q) * GD_HEAD_DIM ** -0.5
    k = l2norm(k)
    beta = jax.nn.sigmoid(beta_logit.astype(f32))
    g = -jnp.exp(p['gd_a_log'].astype(f32)) * jax.nn.softplus(alpha_logit.astype(f32) + p['gd_dt_bias'])
    o, S = gdn_sequence(S0.astype(f32), q, k, v, g, beta, n_lead)
    o = rmsnorm(o, p['gd_norm_w']).reshape(B, T, GD_WIDTH) * jax.nn.silu(gate.astype(f32))
    return o, conv_new, S


def trunk_layer(h, n_lead, shift_prev, conv_prev, s_rw, s_gd, p):
    B, T, _ = h.shape
    xn = rmsnorm(h, p['norm_pre'])
    proj = xn @ p['w_in']
    ps = proj[..., OFF_RW_SHIFT:OFF_RW_GATE]
    prev = jnp.concatenate([shift_prev[:, None, :].astype(ps.dtype), ps[:, :-1]], axis=1)
    xs = ps + p['rw_mu'] * (prev - ps)
    y_rw, s_rw_new = rwkv7_branch(xs, proj[..., OFF_RW_GATE:OFF_GD_CONV], s_rw, p)
    y_gd, conv_new, s_gd_new = gdn_branch(proj[..., OFF_GD_CONV:OFF_GD_BETA], conv_prev,
                                          proj[..., OFF_GD_BETA:OFF_GD_ALPHA], proj[..., OFF_GD_ALPHA:OFF_GD_GATE],
                                          proj[..., OFF_GD_GATE:OFF_MERGE], s_gd, n_lead, p)
    gates = jax.nn.sigmoid(proj[..., OFF_MERGE:].astype(jnp.float32)).reshape(B, T, N_BRANCH, D_MODEL)
    merged = gates[..., 0, :] * (y_rw @ p['w_out_a']) + gates[..., 1, :] * (y_gd @ p['w_out_b'])
    out = (merged @ p['w_out']).astype(h.dtype)
    h = h + rmsnorm(out, p['norm_post'])
    dt = h.dtype
    return h, (ps[:, -1].astype(dt), s_rw_new.astype(dt), conv_new.astype(dt), s_gd_new.astype(dt))


def setup_inputs(seed: int = 0) -> dict:
    key = jax.random.key(seed)
    ks = iter(jax.random.split(key, 32))
    nrm = lambda shape, s: jax.random.normal(next(ks), shape, jnp.float32) * s
    unif = lambda shape, lo, hi: jax.random.uniform(next(ks), shape, jnp.float32, lo, hi)
    Ly = (DEPTH,)
    return {
        'x_prompt': nrm((BATCH, SEQ, D_MODEL), 1.0),
        'x_sample': nrm((DEC_BATCH, DEC_SEQ, D_MODEL), 1.0),
        'state_rwkv_shift': nrm(Ly + (DEC_BATCH, RW_SHIFT_WIDTH), 1.0),
        'state_rwkv_wkv': nrm(Ly + (DEC_BATCH, RW_HEADS, RW_HEAD_DIM, RW_HEAD_DIM), 0.3),
        'state_gdn_conv': nrm(Ly + (DEC_BATCH, GD_CONV - 1, GD_CONV_WIDTH), 1.0),
        'state_gdn_ssm': nrm(Ly + (DEC_BATCH, GD_HEADS, GD_HEAD_DIM, GD_HEAD_DIM), 0.1),
        'meta_tokens': nrm((N_META, D_MODEL), 1.0),
        'norm_pre': 1.0 + nrm(Ly + (D_MODEL,), 0.05),
        'w_in': nrm(Ly + (D_MODEL, PROJ_WIDTH), D_MODEL ** -0.5),
        'rw_mu': unif(Ly + (RW_SHIFT_WIDTH,), 0.0, 1.0),
        'rw_w0': nrm(Ly + (RW_WIDTH,), 0.5),
        'rw_w2': nrm(Ly + (RW_DECAY_RANK, RW_WIDTH), RW_DECAY_RANK ** -0.5),
        'rw_a0': nrm(Ly + (RW_WIDTH,), 0.5),
        'rw_a2': nrm(Ly + (RW_A_RANK, RW_WIDTH), RW_A_RANK ** -0.5),
        'rw_k_k': 0.85 + nrm(Ly + (RW_WIDTH,), 0.05),
        'rw_k_a': 1.0 + nrm(Ly + (RW_WIDTH,), 0.05),
        'rw_r_k': nrm(Ly + (RW_HEADS, RW_HEAD_DIM), 0.1),
        'rw_ln_w': 1.0 + nrm(Ly + (RW_WIDTH,), 0.05),
        'rw_ln_b': nrm(Ly + (RW_WIDTH,), 0.02),
        'gd_conv_w': nrm(Ly + (GD_CONV, GD_CONV_WIDTH), GD_CONV ** -0.5),
        'gd_a_log': jnp.log(unif(Ly + (GD_HEADS,), 1.0, 8.0)),
        'gd_dt_bias': nrm(Ly + (GD_HEADS,), 0.5) - 2.0,
        'gd_norm_w': 1.0 + nrm(Ly + (GD_HEAD_DIM,), 0.05),
        'w_out_a': nrm(Ly + (RW_WIDTH, D_MODEL), RW_WIDTH ** -0.5),
        'w_out_b': nrm(Ly + (GD_WIDTH, D_MODEL), GD_WIDTH ** -0.5),
        'w_out': nrm(Ly + (D_MODEL, D_MODEL), D_MODEL ** -0.5),
        'norm_post': 1.0 + nrm(Ly + (D_MODEL,), 0.05),
    }


def reference(x_prompt, x_sample, state_rwkv_shift, state_rwkv_wkv, state_gdn_conv, state_gdn_ssm,
              meta_tokens, norm_pre, w_in, rw_mu, rw_w0, rw_w2, rw_a0, rw_a2, rw_k_k, rw_k_a, rw_r_k,
              rw_ln_w, rw_ln_b, gd_conv_w, gd_a_log, gd_dt_bias, gd_norm_w, w_out_a, w_out_b, w_out, norm_post):
    Bp = x_prompt.shape[0]
    dtp = x_prompt.dtype
    meta = jnp.broadcast_to(meta_tokens.astype(dtp)[None], (Bp, N_META, D_MODEL))
    hp = jnp.concatenate([meta, x_prompt], axis=1)
    hs = x_sample
    p_states = [[], [], [], []]
    s_states = [[], [], [], []]
    for l in range(DEPTH):
        p = {
            'norm_pre': norm_pre[l], 'w_in': w_in[l], 'rw_mu': rw_mu[l], 'rw_w0': rw_w0[l], 'rw_w2': rw_w2[l],
            'rw_a0': rw_a0[l], 'rw_a2': rw_a2[l], 'rw_k_k': rw_k_k[l], 'rw_k_a': rw_k_a[l], 'rw_r_k': rw_r_k[l],
            'rw_ln_w': rw_ln_w[l], 'rw_ln_b': rw_ln_b[l], 'gd_conv_w': gd_conv_w[l], 'gd_a_log': gd_a_log[l],
            'gd_dt_bias': gd_dt_bias[l], 'gd_norm_w': gd_norm_w[l], 'w_out_a': w_out_a[l], 'w_out_b': w_out_b[l],
            'w_out': w_out[l], 'norm_post': norm_post[l],
        }
        hp, new_p = trunk_layer(
            hp, N_META,
            jnp.zeros((Bp, RW_SHIFT_WIDTH), dtp),
            jnp.zeros((Bp, GD_CONV - 1, GD_CONV_WIDTH), dtp),
            jnp.zeros((Bp, RW_HEADS, RW_HEAD_DIM, RW_HEAD_DIM), jnp.float32),
            jnp.zeros((Bp, GD_HEADS, GD_HEAD_DIM, GD_HEAD_DIM), jnp.float32),
            p)
        hs, new_s = trunk_layer(hs, 0, state_rwkv_shift[l], state_gdn_conv[l], state_rwkv_wkv[l], state_gdn_ssm[l], p)
        for i in range(4):
            p_states[i].append(new_p[i])
            s_states[i].append(new_s[i])
    p_shift, p_wkv, p_conv, p_ssm = (jnp.stack(t, axis=0) for t in p_states)
    s_shift, s_wkv, s_conv, s_ssm = (jnp.stack(t, axis=0) for t in s_states)
    y_prompt = hp[:, N_META:]
    y_sample = hs
    return (y_prompt, y_sample, p_shift, p_wkv, p_conv, p_ssm, s_shift, s_wkv, s_conv, s_ssm)
```

```python
import functools

import numpy as np
import jax
import jax.numpy as jnp
from jax import lax
from jax.experimental import pallas as pl
from jax.experimental.pallas import tpu as pltpu

F32 = jnp.float32
BF16 = jnp.bfloat16

D_MODEL = 1024
N_META = 16
NORM_EPS = 1e-6
RW_HEADS = 16
RW_HEAD_DIM = 64
RW_WIDTH = 1024
RW_LORA = 64
RW_SHIFT_WIDTH = 3 * RW_WIDTH + 2 * RW_LORA
RW_GN_EPS = 64e-5
GD_HEADS = 8
GD_HEAD_DIM = 128
GD_WIDTH = 1024
GD_CONV = 4
GD_CONV_WIDTH = 3 * GD_WIDTH
LANES = 128
SUBLANES = 8
N_PAIRS = RW_HEADS // 2
SEC_RW = RW_SHIFT_WIDTH + RW_WIDTH
SEC_GD = GD_CONV_WIDTH + LANES + GD_WIDTH
SEC_MG = 2 * D_MODEL
PROMPT_CHUNK = 64
INV_BLOCK = 16
VMEM_LIMIT = 56 * 1024 * 1024
ROW_TILE = 512


def _dot(a, b):
    return jnp.dot(a.astype(BF16), b.astype(BF16), preferred_element_type=F32)


def _dot_nt(a, b):
    return lax.dot_general(a.astype(BF16), b.astype(BF16), (((1,), (1,)), ((), ())),
                           preferred_element_type=F32)


def _dot_tn(a, b):
    return lax.dot_general(a.astype(BF16), b.astype(BF16), (((0,), (0,)), ((), ())),
                           preferred_element_type=F32)


def _split3(x):
    hi = x.astype(BF16)
    r1 = x - hi.astype(F32)
    mid = r1.astype(BF16)
    lo = (r1 - mid.astype(F32)).astype(BF16)
    return hi, mid, lo


def _sel_r(x, m01):
    hi, mid, lo = _split3(x)
    d = lambda p: jnp.dot(p, m01, preferred_element_type=F32)
    return d(hi) + d(mid) + d(lo)


def _sel_l(m01, x):
    hi, mid, lo = _split3(x)
    d = lambda p: jnp.dot(m01, p, preferred_element_type=F32)
    return d(hi) + d(mid) + d(lo)


def _sel_tn(x, m01):
    hi, mid, lo = _split3(x)
    d = lambda p: lax.dot_general(p, m01, (((0,), (0,)), ((), ())), preferred_element_type=F32)
    return d(hi) + d(mid) + d(lo)


def _sigmoid(x):
    return 1.0 / (1.0 + jnp.exp(-x))


def _silu(x):
    return x * _sigmoid(x)


def _softplus(x):
    return jnp.maximum(x, 0.0) + jnp.log(1.0 + jnp.exp(-jnp.abs(x)))


def _group_sum(x, e_ref, et_ref):
    return _sel_r(_sel_r(x, e_ref[...]), et_ref[...])


def _log2(n):
    assert n & (n - 1) == 0
    return n.bit_length() - 1


def _tri_masks(n, chunk):
    ii = lax.broadcasted_iota(jnp.int32, (n, n), 0)
    jj = lax.broadcasted_iota(jnp.int32, (n, n), 1)
    ti = ii & (chunk - 1)
    tj = jj & (chunk - 1)
    xor = ii ^ jj
    blk = min(INV_BLOCK, chunk)
    masks = {
        'eye': (ii == jj).astype(F32),
        'strict': ti > tj,
        'incl': ti >= tj,
        'blk': (xor >> _log2(blk)) == 0,
        'off': [],
    }
    s = blk
    while s < chunk:
        masks['off'].append((xor >> _log2(s)) == 1)
        s *= 2
    return masks


def _unit_lower_inverse(n_mat, masks):
    nd = jnp.where(masks['blk'], n_mat, 0.0)
    inv = masks['eye'] + nd
    power = nd
    width = 1
    blk_rows = min(INV_BLOCK, n_mat.shape[0])
    while 2 * width < blk_rows:
        power = _dot(power, power)
        inv = inv + _dot(inv, power)
        width *= 2
    for off_mask in masks['off']:
        off = jnp.where(off_mask, n_mat, 0.0)
        inv = inv + _dot(inv, _dot(off, inv))
    return inv


def _prenorm_kernel(x_ref, g_ref, o_ref):
    x = x_ref[...]
    ms = jnp.mean(x * x, axis=-1, keepdims=True)
    o_ref[...] = (x * lax.rsqrt(ms + NORM_EPS) * g_ref[...]).astype(o_ref.dtype)


def _prenorm(h2d, gain):
    n = h2d.shape[0]
    tm = min(ROW_TILE, n)
    return pl.pallas_call(
        _prenorm_kernel,
        grid=(pl.cdiv(n, tm),),
        in_specs=[pl.BlockSpec((tm, D_MODEL), lambda i: (i, 0)),
                  pl.BlockSpec((1, D_MODEL), lambda i: (0, 0))],
        out_specs=pl.BlockSpec((tm, D_MODEL), lambda i: (i, 0)),
        out_shape=jax.ShapeDtypeStruct((n, D_MODEL), BF16),
        compiler_params=pltpu.CompilerParams(dimension_semantics=("arbitrary",),
                                             vmem_limit_bytes=VMEM_LIMIT),
    )(h2d, gain.reshape(1, D_MODEL))


def _proj_kernel(x_ref, w_ref, o_ref):
    o_ref[...] = jnp.dot(x_ref[...], w_ref[...], preferred_element_type=F32).astype(o_ref.dtype)


def _proj(xn, w):
    n = xn.shape[0]
    width = w.shape[1]
    tm = min(ROW_TILE, n)
    return pl.pallas_call(
        _proj_kernel,
        grid=(pl.cdiv(n, tm),),
        in_specs=[pl.BlockSpec((tm, D_MODEL), lambda i: (i, 0)),
                  pl.BlockSpec((D_MODEL, width), lambda i: (0, 0))],
        out_specs=pl.BlockSpec((tm, width), lambda i: (i, 0)),
        out_shape=jax.ShapeDtypeStruct((n, width), F32),
        compiler_params=pltpu.CompilerParams(dimension_semantics=("arbitrary",),
                                             vmem_limit_bytes=VMEM_LIMIT),
    )(xn, w)


def _rwkv_kernel(x_ref, hist0_ref, s0_ref, mu_ref, w0_ref, a0_ref, kk_ref, ka_ref, rk_ref, lnw_ref,
                 lnb_ref, w2a2_ref, e_ref, et_ref, tril_ref,
                 y_ref, s_ref,
                 hist, at_s, rt_s, bt_s, kt_s, bh_s, kh_s, v_s, el_s, y_s, *, chunk):
    L = chunk
    c = pl.program_id(1)

    @pl.when(c == 0)
    def _():
        hist[...] = hist0_ref[...]
        s_ref[...] = s0_ref[...]

    x = x_ref[...]
    ps = x[:, :RW_SHIFT_WIDTH]
    gate = x[:, RW_SHIFT_WIDTH:]
    full = jnp.concatenate([hist[...], ps], axis=0)
    prev = pltpu.roll(full, 1, axis=0)[SUBLANES:]
    hist[...] = ps[L - SUBLANES:]
    xs = ps + mu_ref[...] * (prev - ps)
    r = xs[:, :RW_WIDTH]
    k = xs[:, RW_WIDTH:2 * RW_WIDTH]
    v = xs[:, 2 * RW_WIDTH:3 * RW_WIDTH]
    z = xs[:, 3 * RW_WIDTH:]
    lane = lax.broadcasted_iota(jnp.int32, (L, LANES), 1)
    lhs = jnp.where(lane < RW_LORA, jnp.tanh(z), z)
    wa = _dot(lhs, w2a2_ref[...])
    w_log = -_softplus(-(w0_ref[...] + wa[:, :RW_WIDTH])) - 0.5
    logd = -jnp.exp(w_log)
    a = _sigmoid(a0_ref[...] + wa[:, RW_WIDTH:])
    kkr = k * kk_ref[...]
    kk = kkr * lax.rsqrt(_group_sum(kkr * kkr, e_ref, et_ref) + 1e-6)
    k2 = k * (1.0 + (a - 1.0) * ka_ref[...])
    bv = kk * a
    lc = _sel_l(tril_ref[...], logd)
    e_pos = jnp.exp(lc)
    e_neg = jnp.exp(-lc)
    e_last = jnp.exp(lc[L - 1:L])
    at = -kk * jnp.exp(lc - logd)
    rt = r * e_pos
    bt = bv * e_neg
    kt = k2 * e_neg
    for p in range(N_PAIRS):
        sl = slice(p * LANES, (p + 1) * LANES)
        at_s[p] = at[:, sl]
        rt_s[p] = rt[:, sl]
        bt_s[p] = bt[:, sl]
        kt_s[p] = kt[:, sl]
        bh_s[p] = bt[:, sl] * e_last[:, sl]
        kh_s[p] = kt[:, sl] * e_last[:, sl]
        v_s[p] = v[:, sl]
        el_s[p] = jnp.broadcast_to(e_last[:, sl], (SUBLANES, LANES))

    n = 2 * L
    masks = _tri_masks(n, L)
    first = lax.broadcasted_iota(jnp.int32, (L, LANES), 1) < RW_HEAD_DIM

    def stack(ref, p):
        xp = ref[p]
        return jnp.concatenate([jnp.where(first, xp, 0.0), jnp.where(first, 0.0, xp)], axis=0)

    def pair_body(p, carry):
        a_st, r_st, b_st, k_st = stack(at_s, p), stack(rt_s, p), stack(bt_s, p), stack(kt_s, p)
        bh_st, kh_st, v_st = stack(bh_s, p), stack(kh_s, p), stack(v_s, p)
        ar = jnp.concatenate([a_st, r_st], axis=0)
        bk = jnp.concatenate([b_st, k_st], axis=0)
        m = _dot_nt(ar, bk)
        n_ab = jnp.where(masks['strict'], m[:n, :n], 0.0)
        a_ak = jnp.where(masks['strict'], m[:n, n:], 0.0)
        a_rb = jnp.where(masks['incl'], m[n:, :n], 0.0)
        a_rk = jnp.where(masks['incl'], m[n:, n:], 0.0)
        t_inv = _unit_lower_inverse(n_ab, masks)
        s_p = s_ref[p]
        ars = _dot_nt(ar, s_p)
        u = _dot(t_inv, ars[:n] + _dot(a_ak, v_st))
        uv = jnp.concatenate([u, v_st], axis=0)
        y_st = ars[n:] + _dot(jnp.concatenate([a_rb, a_rk], axis=1), uv)
        y_s[p] = y_st[:L] + y_st[L:]
        bkh = jnp.concatenate([bh_st, kh_st], axis=0)
        s_ref[p] = s_p * el_s[p][0:1] + _dot_tn(uv, bkh)
        return carry

    lax.fori_loop(0, N_PAIRS, pair_body, 0)

    y = jnp.concatenate([y_s[p] for p in range(N_PAIRS)], axis=1)
    inv_dh = 1.0 / RW_HEAD_DIM
    mean = _group_sum(y, e_ref, et_ref) * inv_dh
    yc = y - mean
    var = _group_sum(yc * yc, e_ref, et_ref) * inv_dh
    yn = yc * lax.rsqrt(var + RW_GN_EPS) * lnw_ref[...] + lnb_ref[...]
    bonus = _group_sum(r * k2 * rk_ref[...], e_ref, et_ref) * v
    y_ref[...] = ((yn + bonus) * _silu(gate)).astype(y_ref.dtype)


def _rwkv(proj_rw, hist0, s0_bd, consts, batch, n_chunks, chunk):
    L = chunk
    rows = batch * n_chunks * L
    row = lambda w: pl.BlockSpec((1, w), lambda b, c: (0, 0))
    full = lambda a: pl.BlockSpec(a.shape, lambda b, c: (0,) * a.ndim)
    pair_scratch = pltpu.VMEM((N_PAIRS, L, LANES), F32)
    return pl.pallas_call(
        functools.partial(_rwkv_kernel, chunk=L),
        grid=(batch, n_chunks),
        in_specs=[pl.BlockSpec((L, SEC_RW), lambda b, c: (b * n_chunks + c, 0)),
                  pl.BlockSpec((None, SUBLANES, RW_SHIFT_WIDTH), lambda b, c: (b, 0, 0)),
                  pl.BlockSpec((None, N_PAIRS, LANES, LANES), lambda b, c: (b, 0, 0, 0)),
                  row(RW_SHIFT_WIDTH)] + [row(RW_WIDTH)] * 7 +
                 [full(consts['w2a2']), full(consts['e_rw']), full(consts['et_rw']), full(consts['tril'])],
        out_specs=[pl.BlockSpec((L, RW_WIDTH), lambda b, c: (b * n_chunks + c, 0)),
                   pl.BlockSpec((None, N_PAIRS, LANES, LANES), lambda b, c: (b, 0, 0, 0))],
        out_shape=[jax.ShapeDtypeStruct((rows, RW_WIDTH), BF16),
                   jax.ShapeDtypeStruct((batch, N_PAIRS, LANES, LANES), F32)],
        scratch_shapes=[pltpu.VMEM((SUBLANES, RW_SHIFT_WIDTH), F32)] + [pair_scratch] * 7 +
                       [pltpu.VMEM((N_PAIRS, SUBLANES, LANES), F32), pair_scratch],
        compiler_params=pltpu.CompilerParams(dimension_semantics=("arbitrary", "arbitrary"),
                                             vmem_limit_bytes=VMEM_LIMIT),
    )(proj_rw, hist0, s0_bd, consts['mu'], consts['w0'], consts['a0'], consts['k_k'], consts['k_a'],
      consts['r_k'], consts['ln_w'], consts['ln_b'], consts['w2a2'], consts['e_rw'], consts['et_rw'],
      consts['tril'])


def _gdn_kernel(x_ref, hist0_ref, s0_ref, cw_ref, alog_ref, dtb_ref, gnw_ref, e_ref, et_ref, eb_ref,
                ea_ref, eal_ref, tril_ref, triu_ref,
                y_ref, s_ref,
                hist, k_s, q_s, v_s, g_s, b_s, grow_s, o_s, *, chunk):
    L = chunk
    c = pl.program_id(1)

    @pl.when(c == 0)
    def _():
        hist[...] = hist0_ref[...]
        s_ref[...] = s0_ref[...]

    x = x_ref[...]
    cin = x[:, :GD_CONV_WIDTH]
    ba = x[:, GD_CONV_WIDTH:GD_CONV_WIDTH + LANES]
    gate = x[:, GD_CONV_WIDTH + LANES:]
    full = jnp.concatenate([hist[...], cin], axis=0)
    hist[...] = cin[L - SUBLANES:]
    cw = cw_ref[...]
    conv = cw[GD_CONV - 1:GD_CONV] * cin
    for d in range(1, GD_CONV):
        conv = conv + cw[GD_CONV - 1 - d:GD_CONV - d] * pltpu.roll(full, d, axis=0)[SUBLANES:]
    act = _silu(conv)
    q = act[:, :GD_WIDTH]
    k = act[:, GD_WIDTH:2 * GD_WIDTH]
    v = act[:, 2 * GD_WIDTH:]
    q = q * lax.rsqrt(_group_sum(q * q, e_ref, et_ref) + 1e-6) * (GD_HEAD_DIM ** -0.5)
    k = k * lax.rsqrt(_group_sum(k * k, e_ref, et_ref) + 1e-6)
    beta = _sigmoid(ba)
    g = -jnp.exp(alog_ref[...]) * _softplus(ba + dtb_ref[...])
    beta_x = _sel_r(beta, eb_ref[...])
    g_x = _sel_r(g, ea_ref[...])
    gcol = _sel_l(tril_ref[...], g_x)
    grow_s[...] = _sel_tn(_sel_r(g, eal_ref[...]), triu_ref[...])
    for h in range(GD_HEADS):
        sl = slice(h * LANES, (h + 1) * LANES)
        k_s[h] = k[:, sl]
        q_s[h] = q[:, sl]
        v_s[h] = v[:, sl]
        g_s[h] = gcol[:, sl]
        b_s[h] = beta_x[:, sl]

    masks = _tri_masks(L, L)

    def head_body(h, carry):
        kh, qh, vh, gc, bx = k_s[h], q_s[h], v_s[h], g_s[h], b_s[h]
        kq = jnp.concatenate([kh, qh], axis=0)
        kkqk = _dot_nt(kq, kh)
        gr = grow_s[pl.ds(pl.multiple_of(h * L, L), L), :]
        dec = jnp.where(masks['incl'], jnp.exp(gc[:, :L] - gr), 0.0)
        m = jnp.where(masks['strict'], bx[:, :L] * dec * kkqk[:L], 0.0)
        t_inv = _unit_lower_inverse(-m, masks)
        s_h = s_ref[h]
        kqs = _dot(kq, s_h)
        eg = jnp.exp(gc)
        delta = _dot(t_inv, bx * (vh - eg * kqs[:L]))
        o_s[h] = eg * kqs[L:] + _dot(kkqk[L:] * dec, delta)
        g_last = gc[L - 1:L]
        s_ref[h] = jnp.exp(g_last) * s_h + _dot_tn(kh * jnp.exp(g_last - gc), delta)
        return carry

    lax.fori_loop(0, GD_HEADS, head_body, 0)

    o = jnp.concatenate([o_s[h] for h in range(GD_HEADS)], axis=1)
    ms = _group_sum(o * o, e_ref, et_ref) * (1.0 / GD_HEAD_DIM)
    on = o * lax.rsqrt(ms + NORM_EPS) * gnw_ref[...]
    y_ref[...] = (on * _silu(gate)).astype(y_ref.dtype)


def _gdn(proj_gd, hist0, s0, consts, batch, n_chunks, chunk):
    L = chunk
    rows = batch * n_chunks * L
    full = lambda a: pl.BlockSpec(a.shape, lambda b, c: (0,) * a.ndim)
    head_scratch = pltpu.VMEM((GD_HEADS, L, LANES), F32)
    names = ['conv_w', 'a_log', 'dt_bias', 'gn_w', 'e_gd', 'et_gd', 'e_beta', 'e_alpha', 'e_alpha_l',
             'tril', 'triu']
    return pl.pallas_call(
        functools.partial(_gdn_kernel, chunk=L),
        grid=(batch, n_chunks),
        in_specs=[pl.BlockSpec((L, SEC_GD), lambda b, c: (b * n_chunks + c, 0)),
                  pl.BlockSpec((None, SUBLANES, GD_CONV_WIDTH), lambda b, c: (b, 0, 0)),
                  pl.BlockSpec((None, GD_HEADS, LANES, LANES), lambda b, c: (b, 0, 0, 0))] +
                 [full(consts[nm]) for nm in names],
        out_specs=[pl.BlockSpec((L, GD_WIDTH), lambda b, c: (b * n_chunks + c, 0)),
                   pl.BlockSpec((None, GD_HEADS, LANES, LANES), lambda b, c: (b, 0, 0, 0))],
        out_shape=[jax.ShapeDtypeStruct((rows, GD_WIDTH), BF16),
                   jax.ShapeDtypeStruct((batch, GD_HEADS, LANES, LANES), F32)],
        scratch_shapes=[pltpu.VMEM((SUBLANES, GD_CONV_WIDTH), F32)] + [head_scratch] * 5 +
                       [pltpu.VMEM((GD_HEADS * L, L), F32), head_scratch],
        compiler_params=pltpu.CompilerParams(dimension_semantics=("arbitrary", "arbitrary"),
                                             vmem_limit_bytes=VMEM_LIMIT),
    )(proj_gd, hist0, s0, *[consts[nm] for nm in names])


def _out_kernel(h_ref, yrw_ref, ygd_ref, mg_ref, woa_ref, wob_ref, wo_ref, g_ref, o_ref):
    mg = mg_ref[...]
    ya = jnp.dot(yrw_ref[...], woa_ref[...], preferred_element_type=F32)
    yb = jnp.dot(ygd_ref[...], wob_ref[...], preferred_element_type=F32)
    merged = _sigmoid(mg[:, :D_MODEL]) * ya + _sigmoid(mg[:, D_MODEL:]) * yb
    out = _dot(merged, wo_ref[...])
    ms = jnp.mean(out * out, axis=-1, keepdims=True)
    o_ref[...] = h_ref[...] + out * lax.rsqrt(ms + NORM_EPS) * g_ref[...]


def _merge_out(h2d, y_rw, y_gd, proj_mg, consts):
    n = h2d.shape[0]
    tm = min(ROW_TILE, n)
    rows = lambda w: pl.BlockSpec((tm, w), lambda i: (i, 0))
    wspec = pl.BlockSpec((D_MODEL, D_MODEL), lambda i: (0, 0))
    return pl.pallas_call(
        _out_kernel,
        grid=(pl.cdiv(n, tm),),
        in_specs=[rows(D_MODEL), rows(RW_WIDTH), rows(GD_WIDTH), rows(SEC_MG), wspec, wspec, wspec,
                  pl.BlockSpec((1, D_MODEL), lambda i: (0, 0))],
        out_specs=rows(D_MODEL),
        out_shape=jax.ShapeDtypeStruct((n, D_MODEL), F32),
        compiler_params=pltpu.CompilerParams(dimension_semantics=("arbitrary",),
                                             vmem_limit_bytes=VMEM_LIMIT),
    )(h2d, y_rw, y_gd, proj_mg, consts['w_oa'], consts['w_ob'], consts['w_o'], consts['norm_post'])


def _indicator(n_groups, group, rows=None):
    rows = rows or LANES
    m = np.zeros((n_groups * group, rows), np.float32)
    for g in range(n_groups):
        m[g * group:(g + 1) * group, g] = 1.0
    return m


def _chunk_consts(chunk):
    L = chunk
    tril = np.tril(np.ones((L, L), np.float32))
    ea_l = np.zeros((LANES, GD_HEADS * L), np.float32)
    for h in range(GD_HEADS):
        ea_l[GD_HEADS + h, h * L:(h + 1) * L] = 1.0
    return {'tril': jnp.asarray(tril, BF16), 'triu': jnp.asarray(tril.T, BF16),
            'e_alpha_l': jnp.asarray(ea_l, BF16)}


def _layer_consts(norm_post, w_in, rw_mu, rw_w0, rw_w2, rw_a0, rw_a2, rw_k_k, rw_k_a, rw_r_k, rw_ln_w,
                  rw_ln_b, gd_conv_w, gd_a_log, gd_dt_bias, gd_norm_w, w_out_a, w_out_b, w_out):
    off_gate = RW_SHIFT_WIDTH
    off_conv = off_gate + RW_WIDTH
    off_beta = off_conv + GD_CONV_WIDTH
    off_ggate = off_beta + 2 * GD_HEADS
    off_merge = off_ggate + GD_WIDTH
    w_bf = w_in.astype(BF16)
    pad = jnp.zeros((D_MODEL, LANES - 2 * GD_HEADS), BF16)
    w2a2 = jnp.zeros((2 * RW_LORA, 2 * RW_WIDTH), F32)
    w2a2 = w2a2.at[:RW_LORA, :RW_WIDTH].set(rw_w2).at[RW_LORA:, RW_WIDTH:].set(rw_a2)
    e_rw = _indicator(RW_HEADS, RW_HEAD_DIM)
    e_gd = _indicator(GD_HEADS, GD_HEAD_DIM)
    e_beta = np.zeros((LANES, GD_WIDTH), np.float32)
    e_alpha = np.zeros((LANES, GD_WIDTH), np.float32)
    for h in range(GD_HEADS):
        e_beta[h, h * GD_HEAD_DIM:(h + 1) * GD_HEAD_DIM] = 1.0
        e_alpha[GD_HEADS + h, h * GD_HEAD_DIM:(h + 1) * GD_HEAD_DIM] = 1.0
    lane_row = lambda vec: jnp.zeros((1, LANES), F32).at[0, GD_HEADS:2 * GD_HEADS].set(vec)
    row = lambda a: a.reshape(1, -1).astype(F32)
    return {
        'w_rw': w_bf[:, :off_conv],
        'w_gd': jnp.concatenate([w_bf[:, off_conv:off_ggate], pad, w_bf[:, off_ggate:off_merge]], axis=1),
        'w_mg': w_bf[:, off_merge:],
        'mu': row(rw_mu), 'w0': row(rw_w0), 'a0': row(rw_a0), 'k_k': row(rw_k_k), 'k_a': row(rw_k_a),
        'r_k': row(rw_r_k), 'ln_w': row(rw_ln_w), 'ln_b': row(rw_ln_b),
        'w2a2': w2a2.astype(BF16),
        'e_rw': jnp.asarray(e_rw, BF16), 'et_rw': jnp.asarray(e_rw.T, BF16),
        'e_gd': jnp.asarray(e_gd, BF16), 'et_gd': jnp.asarray(e_gd.T, BF16),
        'e_beta': jnp.asarray(e_beta, BF16), 'e_alpha': jnp.asarray(e_alpha, BF16),
        'conv_w': gd_conv_w.astype(F32), 'a_log': lane_row(gd_a_log), 'dt_bias': lane_row(gd_dt_bias),
        'gn_w': row(jnp.tile(gd_norm_w, GD_HEADS)),
        'w_oa': w_out_a.astype(BF16), 'w_ob': w_out_b.astype(BF16), 'w_o': w_out.astype(BF16),
        'norm_post': row(norm_post),
    }


def _pair_block_diag(s):
    b = s.shape[0]
    s5 = s.reshape(b, N_PAIRS, 2, RW_HEAD_DIM, RW_HEAD_DIM)
    bd = jnp.einsum('bphvk,hg->bphvgk', s5, jnp.eye(2, dtype=s.dtype))
    return bd.reshape(b, N_PAIRS, LANES, LANES)


def _pair_diag_blocks(bd):
    b = bd.shape[0]
    bd6 = bd.reshape(b, N_PAIRS, 2, RW_HEAD_DIM, 2, RW_HEAD_DIM)
    s5 = jnp.einsum('bphvgk,hg->bphvk', bd6, jnp.eye(2, dtype=bd.dtype))
    return s5.reshape(b, RW_HEADS, RW_HEAD_DIM, RW_HEAD_DIM)


def _trunk(h, chunk, shift_prev, conv_prev, s_rw, s_gd, norm_pre, consts):
    batch, t, _ = h.shape
    n_chunks = t // chunk
    cc = dict(consts, **_chunk_consts(chunk))
    h2d = h.reshape(batch * t, D_MODEL)
    xn = _prenorm(h2d, norm_pre)
    proj_rw = _proj(xn, consts['w_rw'])
    proj_gd = _proj(xn, consts['w_gd'])
    proj_mg = _proj(xn, consts['w_mg'])
    hist_rw = jnp.zeros((batch, SUBLANES, RW_SHIFT_WIDTH), F32).at[:, SUBLANES - 1].set(shift_prev)
    hist_gd = jnp.zeros((batch, SUBLANES, GD_CONV_WIDTH), F32).at[:, SUBLANES - (GD_CONV - 1):].set(conv_prev)
    y_rw, s_rw_bd = _rwkv(proj_rw, hist_rw, _pair_block_diag(s_rw), cc, batch, n_chunks, chunk)
    y_gd, s_gd_new = _gdn(proj_gd, hist_gd, s_gd, cc, batch, n_chunks, chunk)
    h_new = _merge_out(h2d, y_rw, y_gd, proj_mg, consts).reshape(batch, t, D_MODEL)
    p_rw = proj_rw.reshape(batch, t, SEC_RW)
    p_gd = proj_gd.reshape(batch, t, SEC_GD)
    shift_new = p_rw[:, -1, :RW_SHIFT_WIDTH]
    conv_new = p_gd[:, t - (GD_CONV - 1):, :GD_CONV_WIDTH]
    return h_new, (shift_new, _pair_diag_blocks(s_rw_bd), conv_new, s_gd_new)


def kernel(x_prompt, x_sample, state_rwkv_shift, state_rwkv_wkv, state_gdn_conv, state_gdn_ssm, meta_tokens, norm_pre, w_in, rw_mu, rw_w0, rw_w2, rw_a0, rw_a2, rw_k_k, rw_k_a, rw_r_k, rw_ln_w, rw_ln_b, gd_conv_w, gd_a_log, gd_dt_bias, gd_norm_w, w_out_a, w_out_b, w_out, norm_post):
    depth = w_in.shape[0]
    bp, seq, _ = x_prompt.shape
    bs, dec_seq, _ = x_sample.shape
    n_front = (-(N_META + seq)) % PROMPT_CHUNK
    front = jnp.zeros((bp, n_front, D_MODEL), F32)
    meta = jnp.broadcast_to(meta_tokens[None], (bp, N_META, D_MODEL))
    hp = jnp.concatenate([front, meta, x_prompt], axis=1)
    hs = x_sample
    p_states = [[], [], [], []]
    s_states = [[], [], [], []]
    for l in range(depth):
        consts = _layer_consts(norm_post[l], w_in[l], rw_mu[l], rw_w0[l], rw_w2[l], rw_a0[l], rw_a2[l],
                               rw_k_k[l], rw_k_a[l], rw_r_k[l], rw_ln_w[l], rw_ln_b[l], gd_conv_w[l],
                               gd_a_log[l], gd_dt_bias[l], gd_norm_w[l], w_out_a[l], w_out_b[l], w_out[l])
        hp, new_p = _trunk(hp, PROMPT_CHUNK,
                           jnp.zeros((bp, RW_SHIFT_WIDTH), F32),
                           jnp.zeros((bp, GD_CONV - 1, GD_CONV_WIDTH), F32),
                           jnp.zeros((bp, RW_HEADS, RW_HEAD_DIM, RW_HEAD_DIM), F32),
                           jnp.zeros((bp, GD_HEADS, GD_HEAD_DIM, GD_HEAD_DIM), F32),
                           norm_pre[l], consts)
        hs, new_s = _trunk(hs, dec_seq, state_rwkv_shift[l], state_gdn_conv[l], state_rwkv_wkv[l],
                           state_gdn_ssm[l], norm_pre[l], consts)
        for i in range(4):
            p_states[i].append(new_p[i])
            s_states[i].append(new_s[i])
    p_shift, p_wkv, p_conv, p_ssm = (jnp.stack(t, axis=0) for t in p_states)
    s_shift, s_wkv, s_conv, s_ssm = (jnp.stack(t, axis=0) for t in s_states)
    y_prompt = hp[:, n_front + N_META:]
    return (y_prompt, hs, p_shift, p_wkv, p_conv, p_ssm, s_shift, s_wkv, s_conv, s_ssm)
```

```python
import functools

import numpy as np
import jax
import jax.numpy as jnp
from jax import lax
from jax.experimental import pallas as pl
from jax.experimental.pallas import tpu as pltpu

F32 = jnp.float32
BF16 = jnp.bfloat16

D_MODEL = 1024
N_META = 16
NORM_EPS = 1e-6
RW_HEADS = 16
RW_HEAD_DIM = 64
RW_WIDTH = 1024
RW_LORA = 64
RW_SHIFT_WIDTH = 3 * RW_WIDTH + 2 * RW_LORA
RW_GN_EPS = 64e-5
GD_HEADS = 8
GD_HEAD_DIM = 128
GD_WIDTH = 1024
GD_CONV = 4
GD_CONV_WIDTH = 3 * GD_WIDTH
LANES = 128
SUBLANES = 8
N_PAIRS = RW_HEADS // 2
SEC_RW = RW_SHIFT_WIDTH + RW_WIDTH
SEC_GD = GD_CONV_WIDTH + LANES + GD_WIDTH
SEC_MG = 2 * D_MODEL
PROMPT_CHUNK = 64
INV_BLOCK = 16
PAIR_GROUP = 8
HEAD_GROUP = 8
VMEM_LIMIT = 56 * 1024 * 1024
ROW_TILE = 512


def _dot(a, b):
    return jnp.dot(a.astype(BF16), b.astype(BF16), preferred_element_type=F32)


def _dot_nt(a, b):
    return lax.dot_general(a.astype(BF16), b.astype(BF16), (((1,), (1,)), ((), ())),
                           preferred_element_type=F32)


def _dot_tn(a, b):
    return lax.dot_general(a.astype(BF16), b.astype(BF16), (((0,), (0,)), ((), ())),
                           preferred_element_type=F32)


def _split3(x):
    hi = x.astype(BF16)
    r1 = x - hi.astype(F32)
    mid = r1.astype(BF16)
    lo = (r1 - mid.astype(F32)).astype(BF16)
    return hi, mid, lo


def _sel_r(x, m01):
    hi, mid, lo = _split3(x)
    d = lambda p: jnp.dot(p, m01, preferred_element_type=F32)
    return d(hi) + d(mid) + d(lo)


def _sel_l(m01, x):
    hi, mid, lo = _split3(x)
    d = lambda p: jnp.dot(m01, p, preferred_element_type=F32)
    return d(hi) + d(mid) + d(lo)


def _sel_tn(x, m01):
    hi, mid, lo = _split3(x)
    d = lambda p: lax.dot_general(p, m01, (((0,), (0,)), ((), ())), preferred_element_type=F32)
    return d(hi) + d(mid) + d(lo)


def _sigmoid(x):
    return 1.0 / (1.0 + jnp.exp(-x))


def _silu(x):
    return x * _sigmoid(x)


def _softplus(x):
    return jnp.maximum(x, 0.0) + jnp.log(1.0 + jnp.exp(-jnp.abs(x)))


def _group_sum(x, e_ref, et_ref):
    return _sel_r(_sel_r(x, e_ref[...]), et_ref[...])


def _log2(n):
    assert n & (n - 1) == 0
    return n.bit_length() - 1


def _tri_masks(n, chunk):
    ii = lax.broadcasted_iota(jnp.int32, (n, n), 0)
    jj = lax.broadcasted_iota(jnp.int32, (n, n), 1)
    ti = ii & (chunk - 1)
    tj = jj & (chunk - 1)
    xor = ii ^ jj
    blk = min(INV_BLOCK, chunk)
    masks = {
        'eye': (ii == jj).astype(F32),
        'strict': ti > tj,
        'incl': ti >= tj,
        'blk': (xor >> _log2(blk)) == 0,
        'off': [],
    }
    s = blk
    while s < chunk:
        masks['off'].append((xor >> _log2(s)) == 1)
        s *= 2
    return masks


def _unit_lower_inverse(n_mats, masks):
    power = [jnp.where(masks['blk'], x, 0.0) for x in n_mats]
    inv = [masks['eye'] + x for x in power]
    width = 1
    blk_rows = min(INV_BLOCK, n_mats[0].shape[0])
    while 2 * width < blk_rows:
        power = [_dot(x, x) for x in power]
        inv = [i_ + _dot(i_, x) for i_, x in zip(inv, power)]
        width *= 2
    for off_mask in masks['off']:
        off = [jnp.where(off_mask, x, 0.0) for x in n_mats]
        tmp = [_dot(o_, i_) for o_, i_ in zip(off, inv)]
        inv = [i_ + _dot(i_, t_) for i_, t_ in zip(inv, tmp)]
    return inv


def _prenorm_kernel(x_ref, g_ref, o_ref):
    x = x_ref[...]
    ms = jnp.mean(x * x, axis=-1, keepdims=True)
    o_ref[...] = (x * lax.rsqrt(ms + NORM_EPS) * g_ref[...]).astype(o_ref.dtype)


def _prenorm(h2d, gain):
    n = h2d.shape[0]
    tm = min(ROW_TILE, n)
    return pl.pallas_call(
        _prenorm_kernel,
        grid=(pl.cdiv(n, tm),),
        in_specs=[pl.BlockSpec((tm, D_MODEL), lambda i: (i, 0)),
                  pl.BlockSpec((1, D_MODEL), lambda i: (0, 0))],
        out_specs=pl.BlockSpec((tm, D_MODEL), lambda i: (i, 0)),
        out_shape=jax.ShapeDtypeStruct((n, D_MODEL), BF16),
        compiler_params=pltpu.CompilerParams(dimension_semantics=("arbitrary",),
                                             vmem_limit_bytes=VMEM_LIMIT),
    )(h2d, gain.reshape(1, D_MODEL))


def _proj_kernel(x_ref, w_ref, o_ref):
    o_ref[...] = jnp.dot(x_ref[...], w_ref[...], preferred_element_type=F32).astype(o_ref.dtype)


def _proj(xn, w):
    n = xn.shape[0]
    width = w.shape[1]
    tm = min(ROW_TILE, n)
    return pl.pallas_call(
        _proj_kernel,
        grid=(pl.cdiv(n, tm),),
        in_specs=[pl.BlockSpec((tm, D_MODEL), lambda i: (i, 0)),
                  pl.BlockSpec((D_MODEL, width), lambda i: (0, 0))],
        out_specs=pl.BlockSpec((tm, width), lambda i: (i, 0)),
        out_shape=jax.ShapeDtypeStruct((n, width), F32),
        compiler_params=pltpu.CompilerParams(dimension_semantics=("arbitrary",),
                                             vmem_limit_bytes=VMEM_LIMIT),
    )(xn, w)


def _rwkv_kernel(x_ref, hist0_ref, s0_ref, mu_ref, w0_ref, a0_ref, kk_ref, ka_ref, rk_ref, lnw_ref,
                 lnb_ref, w2a2_ref, e_ref, et_ref, tril_ref,
                 y_ref, s_ref, hist, *, chunk):
    L = chunk
    c = pl.program_id(1)

    @pl.when(c == 0)
    def _():
        hist[...] = hist0_ref[...]
        s_ref[...] = s0_ref[...]

    x = x_ref[...]
    ps = x[:, :RW_SHIFT_WIDTH]
    gate = x[:, RW_SHIFT_WIDTH:]
    full = jnp.concatenate([hist[...], ps], axis=0)
    prev = pltpu.roll(full, 1, axis=0)[SUBLANES:]
    hist[...] = ps[L - SUBLANES:]
    xs = ps + mu_ref[...] * (prev - ps)
    r = xs[:, :RW_WIDTH]
    k = xs[:, RW_WIDTH:2 * RW_WIDTH]
    v = xs[:, 2 * RW_WIDTH:3 * RW_WIDTH]
    z = xs[:, 3 * RW_WIDTH:]
    lane = lax.broadcasted_iota(jnp.int32, (L, LANES), 1)
    lhs = jnp.where(lane < RW_LORA, jnp.tanh(z), z)
    wa = _dot(lhs, w2a2_ref[...])
    w_log = -_softplus(-(w0_ref[...] + wa[:, :RW_WIDTH])) - 0.5
    logd = -jnp.exp(w_log)
    a = _sigmoid(a0_ref[...] + wa[:, RW_WIDTH:])
    kkr = k * kk_ref[...]
    kk = kkr * lax.rsqrt(_group_sum(kkr * kkr, e_ref, et_ref) + 1e-6)
    k2 = k * (1.0 + (a - 1.0) * ka_ref[...])
    bv = kk * a
    lc = _sel_l(tril_ref[...], logd)
    e_pos = jnp.exp(lc)
    e_neg = jnp.exp(-lc)
    e_last = jnp.exp(lc[L - 1:L])
    at = -kk * jnp.exp(lc - logd)
    rt = r * e_pos
    bt = bv * e_neg
    kt = k2 * e_neg
    bh = bt * e_last
    kh = kt * e_last

    n = 2 * L
    masks = _tri_masks(n, L)
    first = lax.broadcasted_iota(jnp.int32, (L, LANES), 1) < RW_HEAD_DIM

    def stack(xp):
        return jnp.concatenate([jnp.where(first, xp, 0.0).astype(BF16),
                                jnp.where(first, 0.0, xp).astype(BF16)], axis=0)

    y_parts = []
    for g0 in range(0, N_PAIRS, PAIR_GROUP):
        pairs = range(g0, g0 + PAIR_GROUP)
        sls = [slice(p * LANES, (p + 1) * LANES) for p in pairs]
        v_st = [stack(v[:, sl]) for sl in sls]
        ar = [jnp.concatenate([stack(at[:, sl]), stack(rt[:, sl])], axis=0) for sl in sls]
        bk = [jnp.concatenate([stack(bt[:, sl]), stack(kt[:, sl])], axis=0) for sl in sls]
        bkh = [jnp.concatenate([stack(bh[:, sl]), stack(kh[:, sl])], axis=0) for sl in sls]
        s_old = [s_ref[p] for p in pairs]
        m = [_dot_nt(x, y_) for x, y_ in zip(ar, bk)]
        ars = [_dot_nt(x, s_) for x, s_ in zip(ar, s_old)]
        n_ab = [jnp.where(masks['strict'], x[:n, :n], 0.0) for x in m]
        t_inv = _unit_lower_inverse(n_ab, masks)
        a_ak = [jnp.where(masks['strict'], x[:n, n:], 0.0) for x in m]
        rhs = [x[:n] + _dot(a_, v_) for x, a_, v_ in zip(ars, a_ak, v_st)]
        u = [_dot(t_, x) for t_, x in zip(t_inv, rhs)]
        uv = [jnp.concatenate([u_.astype(BF16), v_], axis=0) for u_, v_ in zip(u, v_st)]
        a_r = [jnp.concatenate([jnp.where(masks['incl'], x[n:, :n], 0.0).astype(BF16),
                                jnp.where(masks['incl'], x[n:, n:], 0.0).astype(BF16)], axis=1) for x in m]
        y_st = [x[n:] + _dot(a_, uv_) for x, a_, uv_ in zip(ars, a_r, uv)]
        y_parts += [x[:L] + x[L:] for x in y_st]
        for p, sl, s_, uv_, bkh_ in zip(pairs, sls, s_old, uv, bkh):
            s_ref[p] = s_ * e_last[:, sl] + _dot_tn(uv_, bkh_)

    y = jnp.concatenate(y_parts, axis=1)
    inv_dh = 1.0 / RW_HEAD_DIM
    mean = _group_sum(y, e_ref, et_ref) * inv_dh
    yc = y - mean
    var = _group_sum(yc * yc, e_ref, et_ref) * inv_dh
    yn = yc * lax.rsqrt(var + RW_GN_EPS) * lnw_ref[...] + lnb_ref[...]
    bonus = _group_sum(r * k2 * rk_ref[...], e_ref, et_ref) * v
    y_ref[...] = ((yn + bonus) * _silu(gate)).astype(y_ref.dtype)


def _rwkv(proj_rw, hist0, s0_bd, consts, batch, n_chunks, chunk):
    L = chunk
    rows = batch * n_chunks * L
    row = lambda w: pl.BlockSpec((1, w), lambda b, c: (0, 0))
    full = lambda a: pl.BlockSpec(a.shape, lambda b, c: (0,) * a.ndim)
    return pl.pallas_call(
        functools.partial(_rwkv_kernel, chunk=L),
        grid=(batch, n_chunks),
        in_specs=[pl.BlockSpec((L, SEC_RW), lambda b, c: (b * n_chunks + c, 0)),
                  pl.BlockSpec((None, SUBLANES, RW_SHIFT_WIDTH), lambda b, c: (b, 0, 0)),
                  pl.BlockSpec((None, N_PAIRS, LANES, LANES), lambda b, c: (b, 0, 0, 0)),
                  row(RW_SHIFT_WIDTH)] + [row(RW_WIDTH)] * 7 +
                 [full(consts['w2a2']), full(consts['e_rw']), full(consts['et_rw']), full(consts['tril'])],
        out_specs=[pl.BlockSpec((L, RW_WIDTH), lambda b, c: (b * n_chunks + c, 0)),
                   pl.BlockSpec((None, N_PAIRS, LANES, LANES), lambda b, c: (b, 0, 0, 0))],
        out_shape=[jax.ShapeDtypeStruct((rows, RW_WIDTH), BF16),
                   jax.ShapeDtypeStruct((batch, N_PAIRS, LANES, LANES), F32)],
        scratch_shapes=[pltpu.VMEM((SUBLANES, RW_SHIFT_WIDTH), F32)],
        compiler_params=pltpu.CompilerParams(dimension_semantics=("arbitrary", "arbitrary"),
                                             vmem_limit_bytes=VMEM_LIMIT),
    )(proj_rw, hist0, s0_bd, consts['mu'], consts['w0'], consts['a0'], consts['k_k'], consts['k_a'],
      consts['r_k'], consts['ln_w'], consts['ln_b'], consts['w2a2'], consts['e_rw'], consts['et_rw'],
      consts['tril'])


def _gdn_kernel(x_ref, hist0_ref, s0_ref, cw_ref, alog_ref, dtb_ref, gnw_ref, e_ref, et_ref, eb_ref,
                ea_ref, eal_ref, tril_ref, triu_ref,
                y_ref, s_ref, hist, *, chunk):
    L = chunk
    c = pl.program_id(1)

    @pl.when(c == 0)
    def _():
        hist[...] = hist0_ref[...]
        s_ref[...] = s0_ref[...]

    x = x_ref[...]
    cin = x[:, :GD_CONV_WIDTH]
    ba = x[:, GD_CONV_WIDTH:GD_CONV_WIDTH + LANES]
    gate = x[:, GD_CONV_WIDTH + LANES:]
    full = jnp.concatenate([hist[...], cin], axis=0)
    hist[...] = cin[L - SUBLANES:]
    cw = cw_ref[...]
    conv = cw[GD_CONV - 1:GD_CONV] * cin
    for d in range(1, GD_CONV):
        conv = conv + cw[GD_CONV - 1 - d:GD_CONV - d] * pltpu.roll(full, d, axis=0)[SUBLANES:]
    act = _silu(conv)
    q = act[:, :GD_WIDTH]
    k = act[:, GD_WIDTH:2 * GD_WIDTH]
    v = act[:, 2 * GD_WIDTH:]
    q = q * lax.rsqrt(_group_sum(q * q, e_ref, et_ref) + 1e-6) * (GD_HEAD_DIM ** -0.5)
    k = k * lax.rsqrt(_group_sum(k * k, e_ref, et_ref) + 1e-6)
    beta = _sigmoid(ba)
    g = -jnp.exp(alog_ref[...]) * _softplus(ba + dtb_ref[...])
    beta_x = _sel_r(beta, eb_ref[...])
    g_x = _sel_r(g, ea_ref[...])
    gcol = _sel_l(tril_ref[...], g_x)
    grow = _sel_tn(_sel_r(g, eal_ref[...]), triu_ref[...])
    masks = _tri_masks(L, L)

    o_parts = []
    for g0 in range(0, GD_HEADS, HEAD_GROUP):
        heads = range(g0, g0 + HEAD_GROUP)
        sls = [slice(h * LANES, (h + 1) * LANES) for h in heads]
        kh = [k[:, sl] for sl in sls]
        gc = [gcol[:, sl] for sl in sls]
        bx = [beta_x[:, sl] for sl in sls]
        kq = [jnp.concatenate([k[:, sl].astype(BF16), q[:, sl].astype(BF16)], axis=0) for sl in sls]
        s_old = [s_ref[h] for h in heads]
        kkqk = [_dot_nt(x, k_) for x, k_ in zip(kq, kh)]
        kqs = [_dot(x, s_) for x, s_ in zip(kq, s_old)]
        dec = [jnp.where(masks['incl'], jnp.exp(g_[:, :L] - grow[h * L:(h + 1) * L]), 0.0)
               for h, g_ in zip(heads, gc)]
        m = [jnp.where(masks['strict'], -(b_[:, :L] * d_ * x[:L]), 0.0) for b_, d_, x in zip(bx, dec, kkqk)]
        t_inv = _unit_lower_inverse(m, masks)
        eg = [jnp.exp(g_) for g_ in gc]
        rhs = [b_ * (v[:, sl] - e_ * x[:L]) for b_, sl, e_, x in zip(bx, sls, eg, kqs)]
        delta = [_dot(t_, x) for t_, x in zip(t_inv, rhs)]
        o_parts += [e_ * x[L:] + _dot(y_[L:] * d_, dl) for e_, x, y_, d_, dl in zip(eg, kqs, kkqk, dec, delta)]
        for h, s_, k_, g_, dl in zip(heads, s_old, kh, gc, delta):
            g_last = g_[L - 1:L]
            s_ref[h] = jnp.exp(g_last) * s_ + _dot_tn(k_ * jnp.exp(g_last - g_), dl)

    o = jnp.concatenate(o_parts, axis=1)
    ms = _group_sum(o * o, e_ref, et_ref) * (1.0 / GD_HEAD_DIM)
    on = o * lax.rsqrt(ms + NORM_EPS) * gnw_ref[...]
    y_ref[...] = (on * _silu(gate)).astype(y_ref.dtype)


def _gdn(proj_gd, hist0, s0, consts, batch, n_chunks, chunk):
    L = chunk
    rows = batch * n_chunks * L
    full = lambda a: pl.BlockSpec(a.shape, lambda b, c: (0,) * a.ndim)
    names = ['conv_w', 'a_log', 'dt_bias', 'gn_w', 'e_gd', 'et_gd', 'e_beta', 'e_alpha', 'e_alpha_l',
             'tril', 'triu']
    return pl.pallas_call(
        functools.partial(_gdn_kernel, chunk=L),
        grid=(batch, n_chunks),
        in_specs=[pl.BlockSpec((L, SEC_GD), lambda b, c: (b * n_chunks + c, 0)),
                  pl.BlockSpec((None, SUBLANES, GD_CONV_WIDTH), lambda b, c: (b, 0, 0)),
                  pl.BlockSpec((None, GD_HEADS, LANES, LANES), lambda b, c: (b, 0, 0, 0))] +
                 [full(consts[nm]) for nm in names],
        out_specs=[pl.BlockSpec((L, GD_WIDTH), lambda b, c: (b * n_chunks + c, 0)),
                   pl.BlockSpec((None, GD_HEADS, LANES, LANES), lambda b, c: (b, 0, 0, 0))],
        out_shape=[jax.ShapeDtypeStruct((rows, GD_WIDTH), BF16),
                   jax.ShapeDtypeStruct((batch, GD_HEADS, LANES, LANES), F32)],
        scratch_shapes=[pltpu.VMEM((SUBLANES, GD_CONV_WIDTH), F32)],
        compiler_params=pltpu.CompilerParams(dimension_semantics=("arbitrary", "arbitrary"),
                                             vmem_limit_bytes=VMEM_LIMIT),
    )(proj_gd, hist0, s0, *[consts[nm] for nm in names])


def _out_kernel(h_ref, yrw_ref, ygd_ref, mg_ref, woa_ref, wob_ref, wo_ref, g_ref, o_ref):
    mg = mg_ref[...]
    ya = jnp.dot(yrw_ref[...], woa_ref[...], preferred_element_type=F32)
    yb = jnp.dot(ygd_ref[...], wob_ref[...], preferred_element_type=F32)
    merged = _sigmoid(mg[:, :D_MODEL]) * ya + _sigmoid(mg[:, D_MODEL:]) * yb
    out = _dot(merged, wo_ref[...])
    ms = jnp.mean(out * out, axis=-1, keepdims=True)
    o_ref[...] = h_ref[...] + out * lax.rsqrt(ms + NORM_EPS) * g_ref[...]


def _merge_out(h2d, y_rw, y_gd, proj_mg, consts):
    n = h2d.shape[0]
    tm = min(ROW_TILE, n)
    rows = lambda w: pl.BlockSpec((tm, w), lambda i: (i, 0))
    wspec = pl.BlockSpec((D_MODEL, D_MODEL), lambda i: (0, 0))
    return pl.pallas_call(
        _out_kernel,
        grid=(pl.cdiv(n, tm),),
        in_specs=[rows(D_MODEL), rows(RW_WIDTH), rows(GD_WIDTH), rows(SEC_MG), wspec, wspec, wspec,
                  pl.BlockSpec((1, D_MODEL), lambda i: (0, 0))],
        out_specs=rows(D_MODEL),
        out_shape=jax.ShapeDtypeStruct((n, D_MODEL), F32),
        compiler_params=pltpu.CompilerParams(dimension_semantics=("arbitrary",),
                                             vmem_limit_bytes=VMEM_LIMIT),
    )(h2d, y_rw, y_gd, proj_mg, consts['w_oa'], consts['w_ob'], consts['w_o'], consts['norm_post'])


def _indicator(n_groups, group, rows=None):
    rows = rows or LANES
    m = np.zeros((n_groups * group, rows), np.float32)
    for g in range(n_groups):
        m[g * group:(g + 1) * group, g] = 1.0
    return m


def _chunk_consts(chunk):
    L = chunk
    tril = np.tril(np.ones((L, L), np.float32))
    ea_l = np.zeros((LANES, GD_HEADS * L), np.float32)
    for h in range(GD_HEADS):
        ea_l[GD_HEADS + h, h * L:(h + 1) * L] = 1.0
    return {'tril': jnp.asarray(tril, BF16), 'triu': jnp.asarray(tril.T, BF16),
            'e_alpha_l': jnp.asarray(ea_l, BF16)}


def _layer_consts(norm_post, w_in, rw_mu, rw_w0, rw_w2, rw_a0, rw_a2, rw_k_k, rw_k_a, rw_r_k, rw_ln_w,
                  rw_ln_b, gd_conv_w, gd_a_log, gd_dt_bias, gd_norm_w, w_out_a, w_out_b, w_out):
    off_gate = RW_SHIFT_WIDTH
    off_conv = off_gate + RW_WIDTH
    off_beta = off_conv + GD_CONV_WIDTH
    off_ggate = off_beta + 2 * GD_HEADS
    off_merge = off_ggate + GD_WIDTH
    w_bf = w_in.astype(BF16)
    pad = jnp.zeros((D_MODEL, LANES - 2 * GD_HEADS), BF16)
    w2a2 = jnp.zeros((2 * RW_LORA, 2 * RW_WIDTH), F32)
    w2a2 = w2a2.at[:RW_LORA, :RW_WIDTH].set(rw_w2).at[RW_LORA:, RW_WIDTH:].set(rw_a2)
    e_rw = _indicator(RW_HEADS, RW_HEAD_DIM)
    e_gd = _indicator(GD_HEADS, GD_HEAD_DIM)
    e_beta = np.zeros((LANES, GD_WIDTH), np.float32)
    e_alpha = np.zeros((LANES, GD_WIDTH), np.float32)
    for h in range(GD_HEADS):
        e_beta[h, h * GD_HEAD_DIM:(h + 1) * GD_HEAD_DIM] = 1.0
        e_alpha[GD_HEADS + h, h * GD_HEAD_DIM:(h + 1) * GD_HEAD_DIM] = 1.0
    lane_row = lambda vec: jnp.zeros((1, LANES), F32).at[0, GD_HEADS:2 * GD_HEADS].set(vec)
    row = lambda a: a.reshape(1, -1).astype(F32)
    return {
        'w_rw': w_bf[:, :off_conv],
        'w_gd': jnp.concatenate([w_bf[:, off_conv:off_ggate], pad, w_bf[:, off_ggate:off_merge]], axis=1),
        'w_mg': w_bf[:, off_merge:],
        'mu': row(rw_mu), 'w0': row(rw_w0), 'a0': row(rw_a0), 'k_k': row(rw_k_k), 'k_a': row(rw_k_a),
        'r_k': row(rw_r_k), 'ln_w': row(rw_ln_w), 'ln_b': row(rw_ln_b),
        'w2a2': w2a2.astype(BF16),
        'e_rw': jnp.asarray(e_rw, BF16), 'et_rw': jnp.asarray(e_rw.T, BF16),
        'e_gd': jnp.asarray(e_gd, BF16), 'et_gd': jnp.asarray(e_gd.T, BF16),
        'e_beta': jnp.asarray(e_beta, BF16), 'e_alpha': jnp.asarray(e_alpha, BF16),
        'conv_w': gd_conv_w.astype(F32), 'a_log': lane_row(gd_a_log), 'dt_bias': lane_row(gd_dt_bias),
        'gn_w': row(jnp.tile(gd_norm_w, GD_HEADS)),
        'w_oa': w_out_a.astype(BF16), 'w_ob': w_out_b.astype(BF16), 'w_o': w_out.astype(BF16),
        'norm_post': row(norm_post),
    }


def _pair_block_diag(s):
    b = s.shape[0]
    s5 = s.reshape(b, N_PAIRS, 2, RW_HEAD_DIM, RW_HEAD_DIM)
    bd = jnp.einsum('bphvk,hg->bphvgk', s5, jnp.eye(2, dtype=s.dtype))
    return bd.reshape(b, N_PAIRS, LANES, LANES)


def _pair_diag_blocks(bd):
    b = bd.shape[0]
    bd6 = bd.reshape(b, N_PAIRS, 2, RW_HEAD_DIM, 2, RW_HEAD_DIM)
    s5 = jnp.einsum('bphvgk,hg->bphvk', bd6, jnp.eye(2, dtype=bd.dtype))
    return s5.reshape(b, RW_HEADS, RW_HEAD_DIM, RW_HEAD_DIM)


def _trunk(h, chunk, shift_prev, conv_prev, s_rw, s_gd, norm_pre, consts):
    batch, t, _ = h.shape
    n_chunks = t // chunk
    cc = dict(consts, **_chunk_consts(chunk))
    h2d = h.reshape(batch * t, D_MODEL)
    xn = _prenorm(h2d, norm_pre)
    proj_rw = _proj(xn, consts['w_rw'])
    proj_gd = _proj(xn, consts['w_gd'])
    proj_mg = _proj(xn, consts['w_mg'])
    hist_rw = jnp.zeros((batch, SUBLANES, RW_SHIFT_WIDTH), F32).at[:, SUBLANES - 1].set(shift_prev)
    hist_gd = jnp.zeros((batch, SUBLANES, GD_CONV_WIDTH), F32).at[:, SUBLANES - (GD_CONV - 1):].set(conv_prev)
    y_rw, s_rw_bd = _rwkv(proj_rw, hist_rw, _pair_block_diag(s_rw), cc, batch, n_chunks, chunk)
    y_gd, s_gd_new = _gdn(proj_gd, hist_gd, s_gd, cc, batch, n_chunks, chunk)
    h_new = _merge_out(h2d, y_rw, y_gd, proj_mg, consts).reshape(batch, t, D_MODEL)
    p_rw = proj_rw.reshape(batch, t, SEC_RW)
    p_gd = proj_gd.reshape(batch, t, SEC_GD)
    shift_new = p_rw[:, -1, :RW_SHIFT_WIDTH]
    conv_new = p_gd[:, t - (GD_CONV - 1):, :GD_CONV_WIDTH]
    return h_new, (shift_new, _pair_diag_blocks(s_rw_bd), conv_new, s_gd_new)


def kernel(x_prompt, x_sample, state_rwkv_shift, state_rwkv_wkv, state_gdn_conv, state_gdn_ssm, meta_tokens, norm_pre, w_in, rw_mu, rw_w0, rw_w2, rw_a0, rw_a2, rw_k_k, rw_k_a, rw_r_k, rw_ln_w, rw_ln_b, gd_conv_w, gd_a_log, gd_dt_bias, gd_norm_w, w_out_a, w_out_b, w_out, norm_post):
    depth = w_in.shape[0]
    bp, seq, _ = x_prompt.shape
    bs, dec_seq, _ = x_sample.shape
    n_front = (-(N_META + seq)) % PROMPT_CHUNK
    front = jnp.zeros((bp, n_front, D_MODEL), F32)
    meta = jnp.broadcast_to(meta_tokens[None], (bp, N_META, D_MODEL))
    hp = jnp.concatenate([front, meta, x_prompt], axis=1)
    hs = x_sample
    p_states = [[], [], [], []]
    s_states = [[], [], [], []]
    for l in range(depth):
        consts = _layer_consts(norm_post[l], w_in[l], rw_mu[l], rw_w0[l], rw_w2[l], rw_a0[l], rw_a2[l],
                               rw_k_k[l], rw_k_a[l], rw_r_k[l], rw_ln_w[l], rw_ln_b[l], gd_conv_w[l],
                               gd_a_log[l], gd_dt_bias[l], gd_norm_w[l], w_out_a[l], w_out_b[l], w_out[l])
        hp, new_p = _trunk(hp, PROMPT_CHUNK,
                           jnp.zeros((bp, RW_SHIFT_WIDTH), F32),
                           jnp.zeros((bp, GD_CONV - 1, GD_CONV_WIDTH), F32),
                           jnp.zeros((bp, RW_HEADS, RW_HEAD_DIM, RW_HEAD_DIM), F32),
                           jnp.zeros((bp, GD_HEADS, GD_HEAD_DIM, GD_HEAD_DIM), F32),
                           norm_pre[l], consts)
        hs, new_s = _trunk(hs, dec_seq, state_rwkv_shift[l], state_gdn_conv[l], state_rwkv_wkv[l],
                           state_gdn_ssm[l], norm_pre[l], consts)
        for i in range(4):
            p_states[i].append(new_p[i])
            s_states[i].append(new_s[i])
    p_shift, p_wkv, p_conv, p_ssm = (jnp.stack(t, axis=0) for t in p_states)
    s_shift, s_wkv, s_conv, s_ssm = (jnp.stack(t, axis=0) for t in s_states)
    y_prompt = hp[:, n_front + N_META:]
    return (y_prompt, hs, p_shift, p_wkv, p_conv, p_ssm, s_shift, s_wkv, s_conv, s_ssm)
```

```python
import functools

import numpy as np
import jax
import jax.numpy as jnp
from jax import lax
from jax.experimental import pallas as pl
from jax.experimental.pallas import tpu as pltpu

F32 = jnp.float32
BF16 = jnp.bfloat16

D_MODEL = 1024
N_META = 16
NORM_EPS = 1e-6
RW_HEADS = 16
RW_HEAD_DIM = 64
RW_WIDTH = 1024
RW_LORA = 64
RW_SHIFT_WIDTH = 3 * RW_WIDTH + 2 * RW_LORA
RW_GN_EPS = 64e-5
GD_HEADS = 8
GD_HEAD_DIM = 128
GD_WIDTH = 1024
GD_CONV = 4
GD_CONV_WIDTH = 3 * GD_WIDTH
LANES = 128
SUBLANES = 8
N_PAIRS = RW_HEADS // 2
SEC_RW = RW_SHIFT_WIDTH + RW_WIDTH
SEC_GD = GD_CONV_WIDTH + LANES + GD_WIDTH
SEC_MG = 2 * D_MODEL
PROMPT_CHUNK = 64
INV_BLOCK = 16
PAIR_GROUP = 8
HEAD_GROUP = 8
VMEM_LIMIT = 56 * 1024 * 1024
ROW_TILE = 512


def _dot(a, b):
    return jnp.dot(a.astype(BF16), b.astype(BF16), preferred_element_type=F32)


def _dot_nt(a, b):
    return lax.dot_general(a.astype(BF16), b.astype(BF16), (((1,), (1,)), ((), ())),
                           preferred_element_type=F32)


def _dot_tn(a, b):
    return lax.dot_general(a.astype(BF16), b.astype(BF16), (((0,), (0,)), ((), ())),
                           preferred_element_type=F32)


def _split3(x):
    hi = x.astype(BF16)
    r1 = x - hi.astype(F32)
    mid = r1.astype(BF16)
    lo = (r1 - mid.astype(F32)).astype(BF16)
    return hi, mid, lo


def _sel_r(x, m01):
    hi, mid, lo = _split3(x)
    d = lambda p: jnp.dot(p, m01, preferred_element_type=F32)
    return d(hi) + d(mid) + d(lo)


def _sel_l(m01, x):
    hi, mid, lo = _split3(x)
    d = lambda p: jnp.dot(m01, p, preferred_element_type=F32)
    return d(hi) + d(mid) + d(lo)


def _sel_tn(x, m01):
    hi, mid, lo = _split3(x)
    d = lambda p: lax.dot_general(p, m01, (((0,), (0,)), ((), ())), preferred_element_type=F32)
    return d(hi) + d(mid) + d(lo)


def _sigmoid(x):
    return 1.0 / (1.0 + jnp.exp(-x))


def _silu(x):
    return x * _sigmoid(x)


def _softplus(x):
    return jnp.maximum(x, 0.0) + jnp.log(1.0 + jnp.exp(-jnp.abs(x)))


def _group_sum(x, group):
    parts = []
    for c in range(x.shape[1] // LANES):
        xs = x[:, c * LANES:(c + 1) * LANES]
        if group == LANES:
            parts.append(jnp.broadcast_to(jnp.sum(xs, axis=-1, keepdims=True), xs.shape))
        else:
            assert 2 * group == LANES
            first = lax.broadcasted_iota(jnp.int32, xs.shape, 1) < group
            s0 = jnp.sum(jnp.where(first, xs, 0.0), axis=-1, keepdims=True)
            s1 = jnp.sum(jnp.where(first, 0.0, xs), axis=-1, keepdims=True)
            parts.append(jnp.where(first, s0, s1))
    return jnp.concatenate(parts, axis=1)


def _log2(n):
    assert n & (n - 1) == 0
    return n.bit_length() - 1


def _tri_masks(n, chunk):
    ii = lax.broadcasted_iota(jnp.int32, (n, n), 0)
    jj = lax.broadcasted_iota(jnp.int32, (n, n), 1)
    ti = ii & (chunk - 1)
    tj = jj & (chunk - 1)
    xor = ii ^ jj
    blk = min(INV_BLOCK, chunk)
    masks = {
        'eye': (ii == jj).astype(F32),
        'strict': ti > tj,
        'incl': ti >= tj,
        'blk': (xor >> _log2(blk)) == 0,
        'off': [],
    }
    s = blk
    while s < chunk:
        masks['off'].append((xor >> _log2(s)) == 1)
        s *= 2
    return masks


def _unit_lower_inverse(n_mats, masks):
    power = [jnp.where(masks['blk'], x, 0.0) for x in n_mats]
    inv = [masks['eye'] + x for x in power]
    width = 1
    blk_rows = min(INV_BLOCK, n_mats[0].shape[0])
    while 2 * width < blk_rows:
        power = [_dot(x, x) for x in power]
        inv = [i_ + _dot(i_, x) for i_, x in zip(inv, power)]
        width *= 2
    for off_mask in masks['off']:
        off = [jnp.where(off_mask, x, 0.0) for x in n_mats]
        tmp = [_dot(o_, i_) for o_, i_ in zip(off, inv)]
        inv = [i_ + _dot(i_, t_) for i_, t_ in zip(inv, tmp)]
    return inv


def _prenorm_kernel(x_ref, g_ref, o_ref):
    x = x_ref[...]
    ms = jnp.mean(x * x, axis=-1, keepdims=True)
    o_ref[...] = (x * lax.rsqrt(ms + NORM_EPS) * g_ref[...]).astype(o_ref.dtype)


def _prenorm(h2d, gain):
    n = h2d.shape[0]
    tm = min(ROW_TILE, n)
    return pl.pallas_call(
        _prenorm_kernel,
        grid=(pl.cdiv(n, tm),),
        in_specs=[pl.BlockSpec((tm, D_MODEL), lambda i: (i, 0)),
                  pl.BlockSpec((1, D_MODEL), lambda i: (0, 0))],
        out_specs=pl.BlockSpec((tm, D_MODEL), lambda i: (i, 0)),
        out_shape=jax.ShapeDtypeStruct((n, D_MODEL), BF16),
        compiler_params=pltpu.CompilerParams(dimension_semantics=("arbitrary",),
                                             vmem_limit_bytes=VMEM_LIMIT),
    )(h2d, gain.reshape(1, D_MODEL))


def _proj_kernel(x_ref, w_ref, o_ref):
    o_ref[...] = jnp.dot(x_ref[...], w_ref[...], preferred_element_type=F32).astype(o_ref.dtype)


def _proj(xn, w):
    n = xn.shape[0]
    width = w.shape[1]
    tm = min(ROW_TILE, n)
    return pl.pallas_call(
        _proj_kernel,
        grid=(pl.cdiv(n, tm),),
        in_specs=[pl.BlockSpec((tm, D_MODEL), lambda i: (i, 0)),
                  pl.BlockSpec((D_MODEL, width), lambda i: (0, 0))],
        out_specs=pl.BlockSpec((tm, width), lambda i: (i, 0)),
        out_shape=jax.ShapeDtypeStruct((n, width), F32),
        compiler_params=pltpu.CompilerParams(dimension_semantics=("arbitrary",),
                                             vmem_limit_bytes=VMEM_LIMIT),
    )(xn, w)


def _rwkv_kernel(x_ref, hist0_ref, s0_ref, mu_ref, w0_ref, a0_ref, kk_ref, ka_ref, rk_ref, lnw_ref,
                 lnb_ref, w2a2_ref, tril_ref,
                 y_ref, s_ref, hist, *, chunk):
    L = chunk
    c = pl.program_id(1)

    @pl.when(c == 0)
    def _():
        hist[...] = hist0_ref[...]
        s_ref[...] = s0_ref[...]

    x = x_ref[...]
    ps = x[:, :RW_SHIFT_WIDTH]
    gate = x[:, RW_SHIFT_WIDTH:]
    full = jnp.concatenate([hist[...], ps], axis=0)
    prev = pltpu.roll(full, 1, axis=0)[SUBLANES:]
    hist[...] = ps[L - SUBLANES:]
    xs = ps + mu_ref[...] * (prev - ps)
    r = xs[:, :RW_WIDTH]
    k = xs[:, RW_WIDTH:2 * RW_WIDTH]
    v = xs[:, 2 * RW_WIDTH:3 * RW_WIDTH]
    z = xs[:, 3 * RW_WIDTH:]
    lane = lax.broadcasted_iota(jnp.int32, (L, LANES), 1)
    lhs = jnp.where(lane < RW_LORA, jnp.tanh(z), z)
    wa = _dot(lhs, w2a2_ref[...])
    w_log = -_softplus(-(w0_ref[...] + wa[:, :RW_WIDTH])) - 0.5
    logd = -jnp.exp(w_log)
    a = _sigmoid(a0_ref[...] + wa[:, RW_WIDTH:])
    kkr = k * kk_ref[...]
    kk = kkr * lax.rsqrt(_group_sum(kkr * kkr, RW_HEAD_DIM) + 1e-6)
    k2 = k * (1.0 + (a - 1.0) * ka_ref[...])
    bv = kk * a
    lc = _sel_l(tril_ref[...], logd)
    e_pos = jnp.exp(lc)
    e_neg = jnp.exp(-lc)
    e_last = jnp.exp(lc[L - 1:L])
    at = -kk * jnp.exp(lc - logd)
    rt = r * e_pos
    bt = bv * e_neg
    kt = k2 * e_neg
    bh = bt * e_last
    kh = kt * e_last

    n = 2 * L
    masks = _tri_masks(n, L)
    first = lax.broadcasted_iota(jnp.int32, (L, LANES), 1) < RW_HEAD_DIM

    def stack(xp):
        return jnp.concatenate([jnp.where(first, xp, 0.0).astype(BF16),
                                jnp.where(first, 0.0, xp).astype(BF16)], axis=0)

    y_parts = []
    for g0 in range(0, N_PAIRS, PAIR_GROUP):
        pairs = range(g0, g0 + PAIR_GROUP)
        sls = [slice(p * LANES, (p + 1) * LANES) for p in pairs]
        v_st = [stack(v[:, sl]) for sl in sls]
        ar = [jnp.concatenate([stack(at[:, sl]), stack(rt[:, sl])], axis=0) for sl in sls]
        bk = [jnp.concatenate([stack(bt[:, sl]), stack(kt[:, sl])], axis=0) for sl in sls]
        bkh = [jnp.concatenate([stack(bh[:, sl]), stack(kh[:, sl])], axis=0) for sl in sls]
        s_old = [s_ref[p] for p in pairs]
        m = [_dot_nt(x, y_) for x, y_ in zip(ar, bk)]
        ars = [_dot_nt(x, s_) for x, s_ in zip(ar, s_old)]
        n_ab = [jnp.where(masks['strict'], x[:n, :n], 0.0) for x in m]
        t_inv = _unit_lower_inverse(n_ab, masks)
        a_ak = [jnp.where(masks['strict'], x[:n, n:], 0.0) for x in m]
        rhs = [x[:n] + _dot(a_, v_) for x, a_, v_ in zip(ars, a_ak, v_st)]
        u = [_dot(t_, x) for t_, x in zip(t_inv, rhs)]
        uv = [jnp.concatenate([u_.astype(BF16), v_], axis=0) for u_, v_ in zip(u, v_st)]
        a_r = [jnp.concatenate([jnp.where(masks['incl'], x[n:, :n], 0.0).astype(BF16),
                                jnp.where(masks['incl'], x[n:, n:], 0.0).astype(BF16)], axis=1) for x in m]
        y_st = [x[n:] + _dot(a_, uv_) for x, a_, uv_ in zip(ars, a_r, uv)]
        y_parts += [x[:L] + x[L:] for x in y_st]
        for p, sl, s_, uv_, bkh_ in zip(pairs, sls, s_old, uv, bkh):
            s_ref[p] = s_ * e_last[:, sl] + _dot_tn(uv_, bkh_)

    y = jnp.concatenate(y_parts, axis=1)
    inv_dh = 1.0 / RW_HEAD_DIM
    mean = _group_sum(y, RW_HEAD_DIM) * inv_dh
    yc = y - mean
    var = _group_sum(yc * yc, RW_HEAD_DIM) * inv_dh
    yn = yc * lax.rsqrt(var + RW_GN_EPS) * lnw_ref[...] + lnb_ref[...]
    bonus = _group_sum(r * k2 * rk_ref[...], RW_HEAD_DIM) * v
    y_ref[...] = ((yn + bonus) * _silu(gate)).astype(y_ref.dtype)


def _rwkv(proj_rw, hist0, s0_bd, consts, batch, n_chunks, chunk):
    L = chunk
    rows = batch * n_chunks * L
    row = lambda w: pl.BlockSpec((1, w), lambda b, c: (0, 0))
    full = lambda a: pl.BlockSpec(a.shape, lambda b, c: (0,) * a.ndim)
    return pl.pallas_call(
        functools.partial(_rwkv_kernel, chunk=L),
        grid=(batch, n_chunks),
        in_specs=[pl.BlockSpec((L, SEC_RW), lambda b, c: (b * n_chunks + c, 0)),
                  pl.BlockSpec((None, SUBLANES, RW_SHIFT_WIDTH), lambda b, c: (b, 0, 0)),
                  pl.BlockSpec((None, N_PAIRS, LANES, LANES), lambda b, c: (b, 0, 0, 0)),
                  row(RW_SHIFT_WIDTH)] + [row(RW_WIDTH)] * 7 +
                 [full(consts['w2a2']), full(consts['tril'])],
        out_specs=[pl.BlockSpec((L, RW_WIDTH), lambda b, c: (b * n_chunks + c, 0)),
                   pl.BlockSpec((None, N_PAIRS, LANES, LANES), lambda b, c: (b, 0, 0, 0))],
        out_shape=[jax.ShapeDtypeStruct((rows, RW_WIDTH), BF16),
                   jax.ShapeDtypeStruct((batch, N_PAIRS, LANES, LANES), F32)],
        scratch_shapes=[pltpu.VMEM((SUBLANES, RW_SHIFT_WIDTH), F32)],
        compiler_params=pltpu.CompilerParams(dimension_semantics=("arbitrary", "arbitrary"),
                                             vmem_limit_bytes=VMEM_LIMIT),
    )(proj_rw, hist0, s0_bd, consts['mu'], consts['w0'], consts['a0'], consts['k_k'], consts['k_a'],
      consts['r_k'], consts['ln_w'], consts['ln_b'], consts['w2a2'], consts['tril'])


def _gdn_kernel(x_ref, hist0_ref, s0_ref, cw_ref, alog_ref, dtb_ref, gnw_ref, eal_ref, tril_ref, triu_ref,
                y_ref, s_ref, hist, *, chunk):
    L = chunk
    c = pl.program_id(1)

    @pl.when(c == 0)
    def _():
        hist[...] = hist0_ref[...]
        s_ref[...] = s0_ref[...]

    x = x_ref[...]
    cin = x[:, :GD_CONV_WIDTH]
    ba = x[:, GD_CONV_WIDTH:GD_CONV_WIDTH + LANES]
    gate = x[:, GD_CONV_WIDTH + LANES:]
    full = jnp.concatenate([hist[...], cin], axis=0)
    hist[...] = cin[L - SUBLANES:]
    cw = cw_ref[...]
    conv = cw[GD_CONV - 1:GD_CONV] * cin
    for d in range(1, GD_CONV):
        conv = conv + cw[GD_CONV - 1 - d:GD_CONV - d] * pltpu.roll(full, d, axis=0)[SUBLANES:]
    act = _silu(conv)
    q = act[:, :GD_WIDTH]
    k = act[:, GD_WIDTH:2 * GD_WIDTH]
    v = act[:, 2 * GD_WIDTH:]
    q = q * lax.rsqrt(_group_sum(q * q, GD_HEAD_DIM) + 1e-6) * (GD_HEAD_DIM ** -0.5)
    k = k * lax.rsqrt(_group_sum(k * k, GD_HEAD_DIM) + 1e-6)
    beta = _sigmoid(ba)
    g = -jnp.exp(alog_ref[...]) * _softplus(ba + dtb_ref[...])
    lane_bcast = lambda t, j: jnp.broadcast_to(t[:, j:j + 1], (L, LANES))
    beta_x = jnp.concatenate([lane_bcast(beta, h) for h in range(GD_HEADS)], axis=1)
    g_x = jnp.concatenate([lane_bcast(g, GD_HEADS + h) for h in range(GD_HEADS)], axis=1)
    gcol = _sel_l(tril_ref[...], g_x)
    grow = _sel_tn(_sel_r(g, eal_ref[...]), triu_ref[...])
    masks = _tri_masks(L, L)

    o_parts = []
    for g0 in range(0, GD_HEADS, HEAD_GROUP):
        heads = range(g0, g0 + HEAD_GROUP)
        sls = [slice(h * LANES, (h + 1) * LANES) for h in heads]
        kh = [k[:, sl] for sl in sls]
        gc = [gcol[:, sl] for sl in sls]
        bx = [beta_x[:, sl] for sl in sls]
        kq = [jnp.concatenate([k[:, sl].astype(BF16), q[:, sl].astype(BF16)], axis=0) for sl in sls]
        s_old = [s_ref[h] for h in heads]
        kkqk = [_dot_nt(x, k_) for x, k_ in zip(kq, kh)]
        kqs = [_dot(x, s_) for x, s_ in zip(kq, s_old)]
        dec = [jnp.where(masks['incl'], jnp.exp(g_[:, :L] - grow[h * L:(h + 1) * L]), 0.0)
               for h, g_ in zip(heads, gc)]
        m = [jnp.where(masks['strict'], -(b_[:, :L] * d_ * x[:L]), 0.0) for b_, d_, x in zip(bx, dec, kkqk)]
        t_inv = _unit_lower_inverse(m, masks)
        eg = [jnp.exp(g_) for g_ in gc]
        rhs = [b_ * (v[:, sl] - e_ * x[:L]) for b_, sl, e_, x in zip(bx, sls, eg, kqs)]
        delta = [_dot(t_, x) for t_, x in zip(t_inv, rhs)]
        o_parts += [e_ * x[L:] + _dot(y_[L:] * d_, dl) for e_, x, y_, d_, dl in zip(eg, kqs, kkqk, dec, delta)]
        for h, s_, k_, g_, dl in zip(heads, s_old, kh, gc, delta):
            g_last = g_[L - 1:L]
            s_ref[h] = jnp.exp(g_last) * s_ + _dot_tn(k_ * jnp.exp(g_last - g_), dl)

    o = jnp.concatenate(o_parts, axis=1)
    ms = _group_sum(o * o, GD_HEAD_DIM) * (1.0 / GD_HEAD_DIM)
    on = o * lax.rsqrt(ms + NORM_EPS) * gnw_ref[...]
    y_ref[...] = (on * _silu(gate)).astype(y_ref.dtype)


def _gdn(proj_gd, hist0, s0, consts, batch, n_chunks, chunk):
    L = chunk
    rows = batch * n_chunks * L
    full = lambda a: pl.BlockSpec(a.shape, lambda b, c: (0,) * a.ndim)
    names = ['conv_w', 'a_log', 'dt_bias', 'gn_w', 'e_alpha_l', 'tril', 'triu']
    return pl.pallas_call(
        functools.partial(_gdn_kernel, chunk=L),
        grid=(batch, n_chunks),
        in_specs=[pl.BlockSpec((L, SEC_GD), lambda b, c: (b * n_chunks + c, 0)),
                  pl.BlockSpec((None, SUBLANES, GD_CONV_WIDTH), lambda b, c: (b, 0, 0)),
                  pl.BlockSpec((None, GD_HEADS, LANES, LANES), lambda b, c: (b, 0, 0, 0))] +
                 [full(consts[nm]) for nm in names],
        out_specs=[pl.BlockSpec((L, GD_WIDTH), lambda b, c: (b * n_chunks + c, 0)),
                   pl.BlockSpec((None, GD_HEADS, LANES, LANES), lambda b, c: (b, 0, 0, 0))],
        out_shape=[jax.ShapeDtypeStruct((rows, GD_WIDTH), BF16),
                   jax.ShapeDtypeStruct((batch, GD_HEADS, LANES, LANES), F32)],
        scratch_shapes=[pltpu.VMEM((SUBLANES, GD_CONV_WIDTH), F32)],
        compiler_params=pltpu.CompilerParams(dimension_semantics=("arbitrary", "arbitrary"),
                                             vmem_limit_bytes=VMEM_LIMIT),
    )(proj_gd, hist0, s0, *[consts[nm] for nm in names])


def _out_kernel(h_ref, yrw_ref, ygd_ref, mg_ref, woa_ref, wob_ref, wo_ref, g_ref, o_ref):
    mg = mg_ref[...]
    ya = jnp.dot(yrw_ref[...], woa_ref[...], preferred_element_type=F32)
    yb = jnp.dot(ygd_ref[...], wob_ref[...], preferred_element_type=F32)
    merged = _sigmoid(mg[:, :D_MODEL]) * ya + _sigmoid(mg[:, D_MODEL:]) * yb
    out = _dot(merged, wo_ref[...])
    ms = jnp.mean(out * out, axis=-1, keepdims=True)
    o_ref[...] = h_ref[...] + out * lax.rsqrt(ms + NORM_EPS) * g_ref[...]


def _merge_out(h2d, y_rw, y_gd, proj_mg, consts):
    n = h2d.shape[0]
    tm = min(ROW_TILE, n)
    rows = lambda w: pl.BlockSpec((tm, w), lambda i: (i, 0))
    wspec = pl.BlockSpec((D_MODEL, D_MODEL), lambda i: (0, 0))
    return pl.pallas_call(
        _out_kernel,
        grid=(pl.cdiv(n, tm),),
        in_specs=[rows(D_MODEL), rows(RW_WIDTH), rows(GD_WIDTH), rows(SEC_MG), wspec, wspec, wspec,
                  pl.BlockSpec((1, D_MODEL), lambda i: (0, 0))],
        out_specs=rows(D_MODEL),
        out_shape=jax.ShapeDtypeStruct((n, D_MODEL), F32),
        compiler_params=pltpu.CompilerParams(dimension_semantics=("arbitrary",),
                                             vmem_limit_bytes=VMEM_LIMIT),
    )(h2d, y_rw, y_gd, proj_mg, consts['w_oa'], consts['w_ob'], consts['w_o'], consts['norm_post'])


def _chunk_consts(chunk):
    L = chunk
    tril = np.tril(np.ones((L, L), np.float32))
    ea_l = np.zeros((LANES, GD_HEADS * L), np.float32)
    for h in range(GD_HEADS):
        ea_l[GD_HEADS + h, h * L:(h + 1) * L] = 1.0
    return {'tril': jnp.asarray(tril, BF16), 'triu': jnp.asarray(tril.T, BF16),
            'e_alpha_l': jnp.asarray(ea_l, BF16)}


def _layer_consts(norm_post, w_in, rw_mu, rw_w0, rw_w2, rw_a0, rw_a2, rw_k_k, rw_k_a, rw_r_k, rw_ln_w,
                  rw_ln_b, gd_conv_w, gd_a_log, gd_dt_bias, gd_norm_w, w_out_a, w_out_b, w_out):
    off_gate = RW_SHIFT_WIDTH
    off_conv = off_gate + RW_WIDTH
    off_beta = off_conv + GD_CONV_WIDTH
    off_ggate = off_beta + 2 * GD_HEADS
    off_merge = off_ggate + GD_WIDTH
    w_bf = w_in.astype(BF16)
    pad = jnp.zeros((D_MODEL, LANES - 2 * GD_HEADS), BF16)
    w2a2 = jnp.zeros((2 * RW_LORA, 2 * RW_WIDTH), F32)
    w2a2 = w2a2.at[:RW_LORA, :RW_WIDTH].set(rw_w2).at[RW_LORA:, RW_WIDTH:].set(rw_a2)
    lane_row = lambda vec: jnp.zeros((1, LANES), F32).at[0, GD_HEADS:2 * GD_HEADS].set(vec)
    row = lambda a: a.reshape(1, -1).astype(F32)
    return {
        'w_rw': w_bf[:, :off_conv],
        'w_gd': jnp.concatenate([w_bf[:, off_conv:off_ggate], pad, w_bf[:, off_ggate:off_merge]], axis=1),
        'w_mg': w_bf[:, off_merge:],
        'mu': row(rw_mu), 'w0': row(rw_w0), 'a0': row(rw_a0), 'k_k': row(rw_k_k), 'k_a': row(rw_k_a),
        'r_k': row(rw_r_k), 'ln_w': row(rw_ln_w), 'ln_b': row(rw_ln_b),
        'w2a2': w2a2.astype(BF16),
        'conv_w': gd_conv_w.astype(F32), 'a_log': lane_row(gd_a_log), 'dt_bias': lane_row(gd_dt_bias),
        'gn_w': row(jnp.tile(gd_norm_w, GD_HEADS)),
        'w_oa': w_out_a.astype(BF16), 'w_ob': w_out_b.astype(BF16), 'w_o': w_out.astype(BF16),
        'norm_post': row(norm_post),
    }


def _pair_block_diag(s):
    b = s.shape[0]
    s5 = s.reshape(b, N_PAIRS, 2, RW_HEAD_DIM, RW_HEAD_DIM)
    bd = jnp.einsum('bphvk,hg->bphvgk', s5, jnp.eye(2, dtype=s.dtype))
    return bd.reshape(b, N_PAIRS, LANES, LANES)


def _pair_diag_blocks(bd):
    b = bd.shape[0]
    bd6 = bd.reshape(b, N_PAIRS, 2, RW_HEAD_DIM, 2, RW_HEAD_DIM)
    s5 = jnp.einsum('bphvgk,hg->bphvk', bd6, jnp.eye(2, dtype=bd.dtype))
    return s5.reshape(b, RW_HEADS, RW_HEAD_DIM, RW_HEAD_DIM)


def _trunk(h, chunk, shift_prev, conv_prev, s_rw, s_gd, norm_pre, consts):
    batch, t, _ = h.shape
    n_chunks = t // chunk
    cc = dict(consts, **_chunk_consts(chunk))
    h2d = h.reshape(batch * t, D_MODEL)
    xn = _prenorm(h2d, norm_pre)
    proj_rw = _proj(xn, consts['w_rw'])
    proj_gd = _proj(xn, consts['w_gd'])
    proj_mg = _proj(xn, consts['w_mg'])
    hist_rw = jnp.zeros((batch, SUBLANES, RW_SHIFT_WIDTH), F32).at[:, SUBLANES - 1].set(shift_prev)
    hist_gd = jnp.zeros((batch, SUBLANES, GD_CONV_WIDTH), F32).at[:, SUBLANES - (GD_CONV - 1):].set(conv_prev)
    y_rw, s_rw_bd = _rwkv(proj_rw, hist_rw, _pair_block_diag(s_rw), cc, batch, n_chunks, chunk)
    y_gd, s_gd_new = _gdn(proj_gd, hist_gd, s_gd, cc, batch, n_chunks, chunk)
    h_new = _merge_out(h2d, y_rw, y_gd, proj_mg, consts).reshape(batch, t, D_MODEL)
    p_rw = proj_rw.reshape(batch, t, SEC_RW)
    p_gd = proj_gd.reshape(batch, t, SEC_GD)
    shift_new = p_rw[:, -1, :RW_SHIFT_WIDTH]
    conv_new = p_gd[:, t - (GD_CONV - 1):, :GD_CONV_WIDTH]
    return h_new, (shift_new, _pair_diag_blocks(s_rw_bd), conv_new, s_gd_new)


def kernel(x_prompt, x_sample, state_rwkv_shift, state_rwkv_wkv, state_gdn_conv, state_gdn_ssm, meta_tokens, norm_pre, w_in, rw_mu, rw_w0, rw_w2, rw_a0, rw_a2, rw_k_k, rw_k_a, rw_r_k, rw_ln_w, rw_ln_b, gd_conv_w, gd_a_log, gd_dt_bias, gd_norm_w, w_out_a, w_out_b, w_out, norm_post):
    depth = w_in.shape[0]
    bp, seq, _ = x_prompt.shape
    bs, dec_seq, _ = x_sample.shape
    n_front = (-(N_META + seq)) % PROMPT_CHUNK
    front = jnp.zeros((bp, n_front, D_MODEL), F32)
    meta = jnp.broadcast_to(meta_tokens[None], (bp, N_META, D_MODEL))
    hp = jnp.concatenate([front, meta, x_prompt], axis=1)
    hs = x_sample
    p_states = [[], [], [], []]
    s_states = [[], [], [], []]
    for l in range(depth):
        consts = _layer_consts(norm_post[l], w_in[l], rw_mu[l], rw_w0[l], rw_w2[l], rw_a0[l], rw_a2[l],
                               rw_k_k[l], rw_k_a[l], rw_r_k[l], rw_ln_w[l], rw_ln_b[l], gd_conv_w[l],
                               gd_a_log[l], gd_dt_bias[l], gd_norm_w[l], w_out_a[l], w_out_b[l], w_out[l])
        hp, new_p = _trunk(hp, PROMPT_CHUNK,
                           jnp.zeros((bp, RW_SHIFT_WIDTH), F32),
                           jnp.zeros((bp, GD_CONV - 1, GD_CONV_WIDTH), F32),
                           jnp.zeros((bp, RW_HEADS, RW_HEAD_DIM, RW_HEAD_DIM), F32),
                           jnp.zeros((bp, GD_HEADS, GD_HEAD_DIM, GD_HEAD_DIM), F32),
                           norm_pre[l], consts)
        hs, new_s = _trunk(hs, dec_seq, state_rwkv_shift[l], state_gdn_conv[l], state_rwkv_wkv[l],
                           state_gdn_ssm[l], norm_pre[l], consts)
        for i in range(4):
            p_states[i].append(new_p[i])
            s_states[i].append(new_s[i])
    p_shift, p_wkv, p_conv, p_ssm = (jnp.stack(t, axis=0) for t in p_states)
    s_shift, s_wkv, s_conv, s_ssm = (jnp.stack(t, axis=0) for t in s_states)
    y_prompt = hp[:, n_front + N_META:]
    return (y_prompt, hs, p_shift, p_wkv, p_conv, p_ssm, s_shift, s_wkv, s_conv, s_ssm)
```

```python
import functools

import numpy as np
import jax
import jax.numpy as jnp
from jax import lax
from jax.experimental import pallas as pl
from jax.experimental.pallas import tpu as pltpu

F32 = jnp.float32
BF16 = jnp.bfloat16

D_MODEL = 1024
N_META = 16
NORM_EPS = 1e-6
RW_HEADS = 16
RW_HEAD_DIM = 64
RW_WIDTH = 1024
RW_LORA = 64
RW_SHIFT_WIDTH = 3 * RW_WIDTH + 2 * RW_LORA
RW_GN_EPS = 64e-5
GD_HEADS = 8
GD_HEAD_DIM = 128
GD_WIDTH = 1024
GD_CONV = 4
GD_CONV_WIDTH = 3 * GD_WIDTH
LANES = 128
SUBLANES = 8
N_PAIRS = RW_HEADS // 2
SEC_RW = RW_SHIFT_WIDTH + RW_WIDTH
SEC_GD = GD_CONV_WIDTH + LANES + GD_WIDTH
SEC_MG = 2 * D_MODEL
PROMPT_CHUNK = 64
INV_BLOCK = 16
PROMPT_SUB = 4
VMEM_LIMIT = 56 * 1024 * 1024
ROW_TILE = 512


def _dot(a, b):
    return jnp.dot(a.astype(BF16), b.astype(BF16), preferred_element_type=F32)


def _dot_nt(a, b):
    return lax.dot_general(a.astype(BF16), b.astype(BF16), (((1,), (1,)), ((), ())),
                           preferred_element_type=F32)


def _dot_tn(a, b):
    return lax.dot_general(a.astype(BF16), b.astype(BF16), (((0,), (0,)), ((), ())),
                           preferred_element_type=F32)


def _split3(x):
    hi = x.astype(BF16)
    r1 = x - hi.astype(F32)
    mid = r1.astype(BF16)
    lo = (r1 - mid.astype(F32)).astype(BF16)
    return hi, mid, lo


def _sel_r(x, m01):
    hi, mid, lo = _split3(x)
    d = lambda p: jnp.dot(p, m01, preferred_element_type=F32)
    return d(hi) + d(mid) + d(lo)


def _sel_l(m01, x):
    hi, mid, lo = _split3(x)
    d = lambda p: jnp.dot(m01, p, preferred_element_type=F32)
    return d(hi) + d(mid) + d(lo)


def _sel_tn(x, m01):
    hi, mid, lo = _split3(x)
    d = lambda p: lax.dot_general(p, m01, (((0,), (0,)), ((), ())), preferred_element_type=F32)
    return d(hi) + d(mid) + d(lo)


def _sigmoid(x):
    return 1.0 / (1.0 + jnp.exp(-x))


def _silu(x):
    return x * _sigmoid(x)


def _softplus(x):
    return jnp.maximum(x, 0.0) + jnp.log(1.0 + jnp.exp(-jnp.abs(x)))


def _group_sum(x, group):
    parts = []
    for c in range(x.shape[1] // LANES):
        xs = x[:, c * LANES:(c + 1) * LANES]
        if group == LANES:
            parts.append(jnp.broadcast_to(jnp.sum(xs, axis=-1, keepdims=True), xs.shape))
        else:
            assert 2 * group == LANES
            first = lax.broadcasted_iota(jnp.int32, xs.shape, 1) < group
            s0 = jnp.sum(jnp.where(first, xs, 0.0), axis=-1, keepdims=True)
            s1 = jnp.sum(jnp.where(first, 0.0, xs), axis=-1, keepdims=True)
            parts.append(jnp.where(first, s0, s1))
    return jnp.concatenate(parts, axis=1)


def _log2(n):
    assert n & (n - 1) == 0
    return n.bit_length() - 1


def _tri_masks(n, chunk):
    ii = lax.broadcasted_iota(jnp.int32, (n, n), 0)
    jj = lax.broadcasted_iota(jnp.int32, (n, n), 1)
    ti = ii & (chunk - 1)
    tj = jj & (chunk - 1)
    xor = ii ^ jj
    blk = min(INV_BLOCK, chunk)
    masks = {
        'eye': (ii == jj).astype(F32),
        'strict': ti > tj,
        'incl': ti >= tj,
        'blk': (xor >> _log2(blk)) == 0,
        'off': [],
    }
    s = blk
    while s < chunk:
        masks['off'].append((xor >> _log2(s)) == 1)
        s *= 2
    return masks


def _unit_lower_inverse(n_mats, masks):
    power = [jnp.where(masks['blk'], x, 0.0) for x in n_mats]
    inv = [masks['eye'] + x for x in power]
    width = 1
    blk_rows = min(INV_BLOCK, n_mats[0].shape[0])
    while 2 * width < blk_rows:
        power = [_dot(x, x) for x in power]
        inv = [i_ + _dot(i_, x) for i_, x in zip(inv, power)]
        width *= 2
    for off_mask in masks['off']:
        off = [jnp.where(off_mask, x, 0.0) for x in n_mats]
        tmp = [_dot(o_, i_) for o_, i_ in zip(off, inv)]
        inv = [i_ + _dot(i_, t_) for i_, t_ in zip(inv, tmp)]
    return inv


def _prenorm_kernel(x_ref, g_ref, o_ref):
    x = x_ref[...]
    ms = jnp.mean(x * x, axis=-1, keepdims=True)
    o_ref[...] = (x * lax.rsqrt(ms + NORM_EPS) * g_ref[...]).astype(o_ref.dtype)


def _prenorm(h2d, gain):
    n = h2d.shape[0]
    tm = min(ROW_TILE, n)
    return pl.pallas_call(
        _prenorm_kernel,
        grid=(pl.cdiv(n, tm),),
        in_specs=[pl.BlockSpec((tm, D_MODEL), lambda i: (i, 0)),
                  pl.BlockSpec((1, D_MODEL), lambda i: (0, 0))],
        out_specs=pl.BlockSpec((tm, D_MODEL), lambda i: (i, 0)),
        out_shape=jax.ShapeDtypeStruct((n, D_MODEL), BF16),
        compiler_params=pltpu.CompilerParams(dimension_semantics=("arbitrary",),
                                             vmem_limit_bytes=VMEM_LIMIT),
    )(h2d, gain.reshape(1, D_MODEL))


def _proj_kernel(x_ref, w_ref, o_ref):
    o_ref[...] = jnp.dot(x_ref[...], w_ref[...], preferred_element_type=F32).astype(o_ref.dtype)


def _proj(xn, w):
    n = xn.shape[0]
    width = w.shape[1]
    tm = min(ROW_TILE, n)
    return pl.pallas_call(
        _proj_kernel,
        grid=(pl.cdiv(n, tm),),
        in_specs=[pl.BlockSpec((tm, D_MODEL), lambda i: (i, 0)),
                  pl.BlockSpec((D_MODEL, width), lambda i: (0, 0))],
        out_specs=pl.BlockSpec((tm, width), lambda i: (i, 0)),
        out_shape=jax.ShapeDtypeStruct((n, width), F32),
        compiler_params=pltpu.CompilerParams(dimension_semantics=("arbitrary",),
                                             vmem_limit_bytes=VMEM_LIMIT),
    )(xn, w)


def _rwkv_kernel(x_ref, hist0_ref, s0_ref, mu_ref, w0_ref, a0_ref, kk_ref, ka_ref, rk_ref, lnw_ref,
                 lnb_ref, w2a2_ref, tril_ref,
                 y_ref, s_ref, hist, *, chunk, n_sub):
    L = chunk
    rows = L * n_sub
    c = pl.program_id(1)

    @pl.when(c == 0)
    def _():
        hist[...] = hist0_ref[...]
        s_ref[...] = s0_ref[...]

    x = x_ref[...]
    ps = x[:, :RW_SHIFT_WIDTH]
    gate = x[:, RW_SHIFT_WIDTH:]
    full = jnp.concatenate([hist[...], ps], axis=0)
    prev = pltpu.roll(full, 1, axis=0)[SUBLANES:]
    hist[...] = ps[rows - SUBLANES:]
    xs = ps + mu_ref[...] * (prev - ps)
    r = xs[:, :RW_WIDTH]
    k = xs[:, RW_WIDTH:2 * RW_WIDTH]
    v = xs[:, 2 * RW_WIDTH:3 * RW_WIDTH]
    z = xs[:, 3 * RW_WIDTH:]
    lane = lax.broadcasted_iota(jnp.int32, (rows, LANES), 1)
    lhs = jnp.where(lane < RW_LORA, jnp.tanh(z), z)
    wa = _dot(lhs, w2a2_ref[...])
    w_log = -_softplus(-(w0_ref[...] + wa[:, :RW_WIDTH])) - 0.5
    logd = -jnp.exp(w_log)
    a = _sigmoid(a0_ref[...] + wa[:, RW_WIDTH:])
    kkr = k * kk_ref[...]
    kk = kkr * lax.rsqrt(_group_sum(kkr * kkr, RW_HEAD_DIM) + 1e-6)
    k2 = k * (1.0 + (a - 1.0) * ka_ref[...])
    bv = kk * a
    lc = _sel_l(tril_ref[...], logd)
    e_pos = jnp.exp(lc)
    e_neg = jnp.exp(-lc)
    e_last = [jnp.exp(lc[(s + 1) * L - 1:(s + 1) * L]) for s in range(n_sub)]
    e_last_rows = jnp.concatenate([jnp.broadcast_to(e, (L, RW_WIDTH)) for e in e_last], axis=0)
    at = -kk * jnp.exp(lc - logd)
    rt = r * e_pos
    bt = bv * e_neg
    kt = k2 * e_neg
    bh = bt * e_last_rows
    kh = kt * e_last_rows

    n = 2 * L
    masks = _tri_masks(n, L)
    first = lax.broadcasted_iota(jnp.int32, (L, LANES), 1) < RW_HEAD_DIM

    def stack(xp):
        return jnp.concatenate([jnp.where(first, xp, 0.0).astype(BF16),
                                jnp.where(first, 0.0, xp).astype(BF16)], axis=0)

    items = [(slice(s * L, (s + 1) * L), slice(p * LANES, (p + 1) * LANES))
             for s in range(n_sub) for p in range(N_PAIRS)]
    v_st = [stack(v[rs, sl]) for rs, sl in items]
    ar = [jnp.concatenate([stack(at[rs, sl]), stack(rt[rs, sl])], axis=0) for rs, sl in items]
    bk = [jnp.concatenate([stack(bt[rs, sl]), stack(kt[rs, sl])], axis=0) for rs, sl in items]
    bkh = [jnp.concatenate([stack(bh[rs, sl]), stack(kh[rs, sl])], axis=0) for rs, sl in items]
    m = [_dot_nt(x_, y_) for x_, y_ in zip(ar, bk)]
    n_ab = [jnp.where(masks['strict'], x_[:n, :n], 0.0) for x_ in m]
    t_inv = _unit_lower_inverse(n_ab, masks)
    akv = [_dot(jnp.where(masks['strict'], x_[:n, n:], 0.0), v_) for x_, v_ in zip(m, v_st)]
    a_r = [jnp.concatenate([jnp.where(masks['incl'], x_[n:, :n], 0.0).astype(BF16),
                            jnp.where(masks['incl'], x_[n:, n:], 0.0).astype(BF16)], axis=1) for x_ in m]
    state = [s_ref[p] for p in range(N_PAIRS)]
    y_rows = []
    for s in range(n_sub):
        it = range(s * N_PAIRS, (s + 1) * N_PAIRS)
        ars = [_dot_nt(ar[i], s_) for i, s_ in zip(it, state)]
        u = [_dot(t_inv[i], x_[:n] + akv[i]) for i, x_ in zip(it, ars)]
        uv = [jnp.concatenate([u_.astype(BF16), v_st[i]], axis=0) for i, u_ in zip(it, u)]
        y_st = [x_[n:] + _dot(a_r[i], uv_) for i, x_, uv_ in zip(it, ars, uv)]
        y_rows.append(jnp.concatenate([x_[:L] + x_[L:] for x_ in y_st], axis=1))
        state = [s_ * e_last[s][:, items[i][1]] + _dot_tn(uv_, bkh[i]) for i, s_, uv_ in zip(it, state, uv)]
    for p in range(N_PAIRS):
        s_ref[p] = state[p]

    y = jnp.concatenate(y_rows, axis=0)
    inv_dh = 1.0 / RW_HEAD_DIM
    mean = _group_sum(y, RW_HEAD_DIM) * inv_dh
    yc = y - mean
    var = _group_sum(yc * yc, RW_HEAD_DIM) * inv_dh
    yn = yc * lax.rsqrt(var + RW_GN_EPS) * lnw_ref[...] + lnb_ref[...]
    bonus = _group_sum(r * k2 * rk_ref[...], RW_HEAD_DIM) * v
    y_ref[...] = ((yn + bonus) * _silu(gate)).astype(y_ref.dtype)


def _rwkv(proj_rw, hist0, s0_bd, consts, batch, n_chunks, chunk, n_sub):
    L = chunk * n_sub
    n_chunks = n_chunks // n_sub
    rows = batch * n_chunks * L
    row = lambda w: pl.BlockSpec((1, w), lambda b, c: (0, 0))
    full = lambda a: pl.BlockSpec(a.shape, lambda b, c: (0,) * a.ndim)
    return pl.pallas_call(
        functools.partial(_rwkv_kernel, chunk=chunk, n_sub=n_sub),
        grid=(batch, n_chunks),
        in_specs=[pl.BlockSpec((L, SEC_RW), lambda b, c: (b * n_chunks + c, 0)),
                  pl.BlockSpec((None, SUBLANES, RW_SHIFT_WIDTH), lambda b, c: (b, 0, 0)),
                  pl.BlockSpec((None, N_PAIRS, LANES, LANES), lambda b, c: (b, 0, 0, 0)),
                  row(RW_SHIFT_WIDTH)] + [row(RW_WIDTH)] * 7 +
                 [full(consts['w2a2']), full(consts['tril'])],
        out_specs=[pl.BlockSpec((L, RW_WIDTH), lambda b, c: (b * n_chunks + c, 0)),
                   pl.BlockSpec((None, N_PAIRS, LANES, LANES), lambda b, c: (b, 0, 0, 0))],
        out_shape=[jax.ShapeDtypeStruct((rows, RW_WIDTH), BF16),
                   jax.ShapeDtypeStruct((batch, N_PAIRS, LANES, LANES), F32)],
        scratch_shapes=[pltpu.VMEM((SUBLANES, RW_SHIFT_WIDTH), F32)],
        compiler_params=pltpu.CompilerParams(dimension_semantics=("arbitrary", "arbitrary"),
                                             vmem_limit_bytes=VMEM_LIMIT),
    )(proj_rw, hist0, s0_bd, consts['mu'], consts['w0'], consts['a0'], consts['k_k'], consts['k_a'],
      consts['r_k'], consts['ln_w'], consts['ln_b'], consts['w2a2'], consts['tril'])


def _gdn_kernel(x_ref, hist0_ref, s0_ref, cw_ref, alog_ref, dtb_ref, gnw_ref, eal_ref, tril_ref, triu_ref,
                y_ref, s_ref, hist, *, chunk, n_sub):
    L = chunk
    rows = L * n_sub
    c = pl.program_id(1)

    @pl.when(c == 0)
    def _():
        hist[...] = hist0_ref[...]
        s_ref[...] = s0_ref[...]

    x = x_ref[...]
    cin = x[:, :GD_CONV_WIDTH]
    ba = x[:, GD_CONV_WIDTH:GD_CONV_WIDTH + LANES]
    gate = x[:, GD_CONV_WIDTH + LANES:]
    full = jnp.concatenate([hist[...], cin], axis=0)
    hist[...] = cin[rows - SUBLANES:]
    cw = cw_ref[...]
    conv = cw[GD_CONV - 1:GD_CONV] * cin
    for d in range(1, GD_CONV):
        conv = conv + cw[GD_CONV - 1 - d:GD_CONV - d] * pltpu.roll(full, d, axis=0)[SUBLANES:]
    act = _silu(conv)
    q = act[:, :GD_WIDTH]
    k = act[:, GD_WIDTH:2 * GD_WIDTH]
    v = act[:, 2 * GD_WIDTH:]
    q = q * lax.rsqrt(_group_sum(q * q, GD_HEAD_DIM) + 1e-6) * (GD_HEAD_DIM ** -0.5)
    k = k * lax.rsqrt(_group_sum(k * k, GD_HEAD_DIM) + 1e-6)
    beta = _sigmoid(ba)
    g = -jnp.exp(alog_ref[...]) * _softplus(ba + dtb_ref[...])
    lane_bcast = lambda t, j: jnp.broadcast_to(t[:, j:j + 1], (rows, LANES))
    beta_x = jnp.concatenate([lane_bcast(beta, h) for h in range(GD_HEADS)], axis=1)
    g_x = jnp.concatenate([lane_bcast(g, GD_HEADS + h) for h in range(GD_HEADS)], axis=1)
    gcol = _sel_l(tril_ref[...], g_x)
    g_l = _sel_r(g, eal_ref[...])
    grow = [_sel_tn(g_l[s * L:(s + 1) * L], triu_ref[...]) for s in range(n_sub)]
    masks = _tri_masks(L, L)
    eg_all = jnp.exp(gcol)

    items = [(s, h, slice(s * L, (s + 1) * L), slice(h * LANES, (h + 1) * LANES))
             for s in range(n_sub) for h in range(GD_HEADS)]
    kq = [jnp.concatenate([k[rs, sl].astype(BF16), q[rs, sl].astype(BF16)], axis=0) for _, _, rs, sl in items]
    kkqk = [_dot_nt(x_, k[rs, sl]) for x_, (_, _, rs, sl) in zip(kq, items)]
    dec = [jnp.where(masks['incl'], jnp.exp(gcol[rs, sl][:, :L] - grow[s][h * L:(h + 1) * L]), 0.0)
           for s, h, rs, sl in items]
    m = [jnp.where(masks['strict'], -(beta_x[rs, sl][:, :L] * d_ * x_[:L]), 0.0)
         for d_, x_, (_, _, rs, sl) in zip(dec, kkqk, items)]
    t_inv = _unit_lower_inverse(m, masks)
    qkd = [(x_[L:] * d_).astype(BF16) for x_, d_ in zip(kkqk, dec)]
    state = [s_ref[h] for h in range(GD_HEADS)]
    o_rows = []
    for s in range(n_sub):
        it = range(s * GD_HEADS, (s + 1) * GD_HEADS)
        kqs = [_dot(kq[i], s_) for i, s_ in zip(it, state)]
        delta = []
        for i, x_ in zip(it, kqs):
            _, _, rs, sl = items[i]
            delta.append(_dot(t_inv[i], beta_x[rs, sl] * (v[rs, sl] - eg_all[rs, sl] * x_[:L])))
        o_rows.append(jnp.concatenate(
            [eg_all[items[i][2], items[i][3]] * x_[L:] + _dot(qkd[i], dl) for i, x_, dl in zip(it, kqs, delta)],
            axis=1))
        new_state = []
        for i, s_, dl in zip(it, state, delta):
            _, _, rs, sl = items[i]
            g_c = gcol[rs, sl]
            g_last = g_c[L - 1:L]
            new_state.append(jnp.exp(g_last) * s_ + _dot_tn(k[rs, sl] * jnp.exp(g_last - g_c), dl))
        state = new_state
    for h in range(GD_HEADS):
        s_ref[h] = state[h]

    o = jnp.concatenate(o_rows, axis=0)
    ms = _group_sum(o * o, GD_HEAD_DIM) * (1.0 / GD_HEAD_DIM)
    on = o * lax.rsqrt(ms + NORM_EPS) * gnw_ref[...]
    y_ref[...] = (on * _silu(gate)).astype(y_ref.dtype)


def _gdn(proj_gd, hist0, s0, consts, batch, n_chunks, chunk, n_sub):
    L = chunk * n_sub
    n_chunks = n_chunks // n_sub
    rows = batch * n_chunks * L
    full = lambda a: pl.BlockSpec(a.shape, lambda b, c: (0,) * a.ndim)
    names = ['conv_w', 'a_log', 'dt_bias', 'gn_w', 'e_alpha_l', 'tril', 'triu']
    return pl.pallas_call(
        functools.partial(_gdn_kernel, chunk=chunk, n_sub=n_sub),
        grid=(batch, n_chunks),
        in_specs=[pl.BlockSpec((L, SEC_GD), lambda b, c: (b * n_chunks + c, 0)),
                  pl.BlockSpec((None, SUBLANES, GD_CONV_WIDTH), lambda b, c: (b, 0, 0)),
                  pl.BlockSpec((None, GD_HEADS, LANES, LANES), lambda b, c: (b, 0, 0, 0))] +
                 [full(consts[nm]) for nm in names],
        out_specs=[pl.BlockSpec((L, GD_WIDTH), lambda b, c: (b * n_chunks + c, 0)),
                   pl.BlockSpec((None, GD_HEADS, LANES, LANES), lambda b, c: (b, 0, 0, 0))],
        out_shape=[jax.ShapeDtypeStruct((rows, GD_WIDTH), BF16),
                   jax.ShapeDtypeStruct((batch, GD_HEADS, LANES, LANES), F32)],
        scratch_shapes=[pltpu.VMEM((SUBLANES, GD_CONV_WIDTH), F32)],
        compiler_params=pltpu.CompilerParams(dimension_semantics=("arbitrary", "arbitrary"),
                                             vmem_limit_bytes=VMEM_LIMIT),
    )(proj_gd, hist0, s0, *[consts[nm] for nm in names])


def _out_kernel(h_ref, yrw_ref, ygd_ref, mg_ref, woa_ref, wob_ref, wo_ref, g_ref, o_ref):
    mg = mg_ref[...]
    ya = jnp.dot(yrw_ref[...], woa_ref[...], preferred_element_type=F32)
    yb = jnp.dot(ygd_ref[...], wob_ref[...], preferred_element_type=F32)
    merged = _sigmoid(mg[:, :D_MODEL]) * ya + _sigmoid(mg[:, D_MODEL:]) * yb
    out = _dot(merged, wo_ref[...])
    ms = jnp.mean(out * out, axis=-1, keepdims=True)
    o_ref[...] = h_ref[...] + out * lax.rsqrt(ms + NORM_EPS) * g_ref[...]


def _merge_out(h2d, y_rw, y_gd, proj_mg, consts):
    n = h2d.shape[0]
    tm = min(ROW_TILE, n)
    rows = lambda w: pl.BlockSpec((tm, w), lambda i: (i, 0))
    wspec = pl.BlockSpec((D_MODEL, D_MODEL), lambda i: (0, 0))
    return pl.pallas_call(
        _out_kernel,
        grid=(pl.cdiv(n, tm),),
        in_specs=[rows(D_MODEL), rows(RW_WIDTH), rows(GD_WIDTH), rows(SEC_MG), wspec, wspec, wspec,
                  pl.BlockSpec((1, D_MODEL), lambda i: (0, 0))],
        out_specs=rows(D_MODEL),
        out_shape=jax.ShapeDtypeStruct((n, D_MODEL), F32),
        compiler_params=pltpu.CompilerParams(dimension_semantics=("arbitrary",),
                                             vmem_limit_bytes=VMEM_LIMIT),
    )(h2d, y_rw, y_gd, proj_mg, consts['w_oa'], consts['w_ob'], consts['w_o'], consts['norm_post'])


def _chunk_consts(chunk, n_sub):
    L = chunk
    tril = np.tril(np.ones((L, L), np.float32))
    ea_l = np.zeros((LANES, GD_HEADS * L), np.float32)
    for h in range(GD_HEADS):
        ea_l[GD_HEADS + h, h * L:(h + 1) * L] = 1.0
    return {'tril': jnp.asarray(np.kron(np.eye(n_sub, dtype=np.float32), tril), BF16),
            'triu': jnp.asarray(tril.T, BF16), 'e_alpha_l': jnp.asarray(ea_l, BF16)}


def _layer_consts(norm_post, w_in, rw_mu, rw_w0, rw_w2, rw_a0, rw_a2, rw_k_k, rw_k_a, rw_r_k, rw_ln_w,
                  rw_ln_b, gd_conv_w, gd_a_log, gd_dt_bias, gd_norm_w, w_out_a, w_out_b, w_out):
    off_gate = RW_SHIFT_WIDTH
    off_conv = off_gate + RW_WIDTH
    off_beta = off_conv + GD_CONV_WIDTH
    off_ggate = off_beta + 2 * GD_HEADS
    off_merge = off_ggate + GD_WIDTH
    w_bf = w_in.astype(BF16)
    pad = jnp.zeros((D_MODEL, LANES - 2 * GD_HEADS), BF16)
    w2a2 = jnp.zeros((2 * RW_LORA, 2 * RW_WIDTH), F32)
    w2a2 = w2a2.at[:RW_LORA, :RW_WIDTH].set(rw_w2).at[RW_LORA:, RW_WIDTH:].set(rw_a2)
    lane_row = lambda vec: jnp.zeros((1, LANES), F32).at[0, GD_HEADS:2 * GD_HEADS].set(vec)
    row = lambda a: a.reshape(1, -1).astype(F32)
    return {
        'w_rw': w_bf[:, :off_conv],
        'w_gd': jnp.concatenate([w_bf[:, off_conv:off_ggate], pad, w_bf[:, off_ggate:off_merge]], axis=1),
        'w_mg': w_bf[:, off_merge:],
        'mu': row(rw_mu), 'w0': row(rw_w0), 'a0': row(rw_a0), 'k_k': row(rw_k_k), 'k_a': row(rw_k_a),
        'r_k': row(rw_r_k), 'ln_w': row(rw_ln_w), 'ln_b': row(rw_ln_b),
        'w2a2': w2a2.astype(BF16),
        'conv_w': gd_conv_w.astype(F32), 'a_log': lane_row(gd_a_log), 'dt_bias': lane_row(gd_dt_bias),
        'gn_w': row(jnp.tile(gd_norm_w, GD_HEADS)),
        'w_oa': w_out_a.astype(BF16), 'w_ob': w_out_b.astype(BF16), 'w_o': w_out.astype(BF16),
        'norm_post': row(norm_post),
    }


def _pair_block_diag(s):
    b = s.shape[0]
    s5 = s.reshape(b, N_PAIRS, 2, RW_HEAD_DIM, RW_HEAD_DIM)
    bd = jnp.einsum('bphvk,hg->bphvgk', s5, jnp.eye(2, dtype=s.dtype))
    return bd.reshape(b, N_PAIRS, LANES, LANES)


def _pair_diag_blocks(bd):
    b = bd.shape[0]
    bd6 = bd.reshape(b, N_PAIRS, 2, RW_HEAD_DIM, 2, RW_HEAD_DIM)
    s5 = jnp.einsum('bphvgk,hg->bphvk', bd6, jnp.eye(2, dtype=bd.dtype))
    return s5.reshape(b, RW_HEADS, RW_HEAD_DIM, RW_HEAD_DIM)


def _trunk(h, chunk, n_sub, shift_prev, conv_prev, s_rw, s_gd, norm_pre, consts):
    batch, t, _ = h.shape
    n_chunks = t // chunk
    cc = dict(consts, **_chunk_consts(chunk, n_sub))
    h2d = h.reshape(batch * t, D_MODEL)
    xn = _prenorm(h2d, norm_pre)
    proj_rw = _proj(xn, consts['w_rw'])
    proj_gd = _proj(xn, consts['w_gd'])
    proj_mg = _proj(xn, consts['w_mg'])
    hist_rw = jnp.zeros((batch, SUBLANES, RW_SHIFT_WIDTH), F32).at[:, SUBLANES - 1].set(shift_prev)
    hist_gd = jnp.zeros((batch, SUBLANES, GD_CONV_WIDTH), F32).at[:, SUBLANES - (GD_CONV - 1):].set(conv_prev)
    y_rw, s_rw_bd = _rwkv(proj_rw, hist_rw, _pair_block_diag(s_rw), cc, batch, n_chunks, chunk, n_sub)
    y_gd, s_gd_new = _gdn(proj_gd, hist_gd, s_gd, cc, batch, n_chunks, chunk, n_sub)
    h_new = _merge_out(h2d, y_rw, y_gd, proj_mg, consts).reshape(batch, t, D_MODEL)
    p_rw = proj_rw.reshape(batch, t, SEC_RW)
    p_gd = proj_gd.reshape(batch, t, SEC_GD)
    shift_new = p_rw[:, -1, :RW_SHIFT_WIDTH]
    conv_new = p_gd[:, t - (GD_CONV - 1):, :GD_CONV_WIDTH]
    return h_new, (shift_new, _pair_diag_blocks(s_rw_bd), conv_new, s_gd_new)


def kernel(x_prompt, x_sample, state_rwkv_shift, state_rwkv_wkv, state_gdn_conv, state_gdn_ssm, meta_tokens, norm_pre, w_in, rw_mu, rw_w0, rw_w2, rw_a0, rw_a2, rw_k_k, rw_k_a, rw_r_k, rw_ln_w, rw_ln_b, gd_conv_w, gd_a_log, gd_dt_bias, gd_norm_w, w_out_a, w_out_b, w_out, norm_post):
    depth = w_in.shape[0]
    bp, seq, _ = x_prompt.shape
    bs, dec_seq, _ = x_sample.shape
    n_front = (-(N_META + seq)) % (PROMPT_CHUNK * PROMPT_SUB)
    front = jnp.zeros((bp, n_front, D_MODEL), F32)
    meta = jnp.broadcast_to(meta_tokens[None], (bp, N_META, D_MODEL))
    hp = jnp.concatenate([front, meta, x_prompt], axis=1)
    hs = x_sample
    p_states = [[], [], [], []]
    s_states = [[], [], [], []]
    for l in range(depth):
        consts = _layer_consts(norm_post[l], w_in[l], rw_mu[l], rw_w0[l], rw_w2[l], rw_a0[l], rw_a2[l],
                               rw_k_k[l], rw_k_a[l], rw_r_k[l], rw_ln_w[l], rw_ln_b[l], gd_conv_w[l],
                               gd_a_log[l], gd_dt_bias[l], gd_norm_w[l], w_out_a[l], w_out_b[l], w_out[l])
        hp, new_p = _trunk(hp, PROMPT_CHUNK, PROMPT_SUB,
                           jnp.zeros((bp, RW_SHIFT_WIDTH), F32),
                           jnp.zeros((bp, GD_CONV - 1, GD_CONV_WIDTH), F32),
                           jnp.zeros((bp, RW_HEADS, RW_HEAD_DIM, RW_HEAD_DIM), F32),
                           jnp.zeros((bp, GD_HEADS, GD_HEAD_DIM, GD_HEAD_DIM), F32),
                           norm_pre[l], consts)
        hs, new_s = _trunk(hs, dec_seq, 1, state_rwkv_shift[l], state_gdn_conv[l], state_rwkv_wkv[l],
                           state_gdn_ssm[l], norm_pre[l], consts)
        for i in range(4):
            p_states[i].append(new_p[i])
            s_states[i].append(new_s[i])
    p_shift, p_wkv, p_conv, p_ssm = (jnp.stack(t, axis=0) for t in p_states)
    s_shift, s_wkv, s_conv, s_ssm = (jnp.stack(t, axis=0) for t in s_states)
    y_prompt = hp[:, n_front + N_META:]
    return (y_prompt, hs, p_shift, p_wkv, p_conv, p_ssm, s_shift, s_wkv, s_conv, s_ssm)
```

```python
import functools

import numpy as np
import jax
import jax.numpy as jnp
from jax import lax
from jax.experimental import pallas as pl
from jax.experimental.pallas import tpu as pltpu

F32 = jnp.float32
BF16 = jnp.bfloat16

D_MODEL = 1024
N_META = 16
NORM_EPS = 1e-6
RW_HEADS = 16
RW_HEAD_DIM = 64
RW_WIDTH = 1024
RW_LORA = 64
RW_SHIFT_WIDTH = 3 * RW_WIDTH + 2 * RW_LORA
RW_GN_EPS = 64e-5
GD_HEADS = 8
GD_HEAD_DIM = 128
GD_WIDTH = 1024
GD_CONV = 4
GD_CONV_WIDTH = 3 * GD_WIDTH
LANES = 128
SUBLANES = 8
N_PAIRS = RW_HEADS // 2
SEC_RW = RW_SHIFT_WIDTH + RW_WIDTH
SEC_GD = GD_CONV_WIDTH + LANES + GD_WIDTH
SEC_MG = 2 * D_MODEL
PROMPT_CHUNK = 64
INV_BLOCK = 16
PROMPT_SUB = 4
VMEM_LIMIT = 56 * 1024 * 1024
ROW_TILE = 512


def _dot(a, b):
    return jnp.dot(a.astype(BF16), b.astype(BF16), preferred_element_type=F32)


def _dot_nt(a, b):
    return lax.dot_general(a.astype(BF16), b.astype(BF16), (((1,), (1,)), ((), ())),
                           preferred_element_type=F32)


def _dot_tn(a, b):
    return lax.dot_general(a.astype(BF16), b.astype(BF16), (((0,), (0,)), ((), ())),
                           preferred_element_type=F32)


def _split3(x):
    hi = x.astype(BF16)
    r1 = x - hi.astype(F32)
    mid = r1.astype(BF16)
    lo = (r1 - mid.astype(F32)).astype(BF16)
    return hi, mid, lo


def _sel_r(x, m01):
    hi, mid, lo = _split3(x)
    d = lambda p: jnp.dot(p, m01, preferred_element_type=F32)
    return d(hi) + d(mid) + d(lo)


def _sel_l(m01, x):
    hi, mid, lo = _split3(x)
    d = lambda p: jnp.dot(m01, p, preferred_element_type=F32)
    return d(hi) + d(mid) + d(lo)


def _sel_tn(x, m01):
    hi, mid, lo = _split3(x)
    d = lambda p: lax.dot_general(p, m01, (((0,), (0,)), ((), ())), preferred_element_type=F32)
    return d(hi) + d(mid) + d(lo)


def _sigmoid(x):
    return 1.0 / (1.0 + jnp.exp(-x))


def _silu(x):
    return x * _sigmoid(x)


def _softplus(x):
    return jnp.maximum(x, 0.0) + jnp.log(1.0 + jnp.exp(-jnp.abs(x)))


def _group_sum(x, group):
    parts = []
    for c in range(x.shape[1] // LANES):
        xs = x[:, c * LANES:(c + 1) * LANES]
        if group == LANES:
            parts.append(jnp.broadcast_to(jnp.sum(xs, axis=-1, keepdims=True), xs.shape))
        else:
            assert 2 * group == LANES
            first = lax.broadcasted_iota(jnp.int32, xs.shape, 1) < group
            s0 = jnp.sum(jnp.where(first, xs, 0.0), axis=-1, keepdims=True)
            s1 = jnp.sum(jnp.where(first, 0.0, xs), axis=-1, keepdims=True)
            parts.append(jnp.where(first, s0, s1))
    return jnp.concatenate(parts, axis=1)


def _log2(n):
    assert n & (n - 1) == 0
    return n.bit_length() - 1


def _tri_masks(n, chunk):
    ii = lax.broadcasted_iota(jnp.int32, (n, n), 0)
    jj = lax.broadcasted_iota(jnp.int32, (n, n), 1)
    ti = ii & (chunk - 1)
    tj = jj & (chunk - 1)
    xor = ii ^ jj
    blk = min(INV_BLOCK, chunk)
    masks = {
        'eye': (ii == jj).astype(F32),
        'strict': ti > tj,
        'incl': ti >= tj,
        'blk': (xor >> _log2(blk)) == 0,
        'off': [],
    }
    s = blk
    while s < chunk:
        masks['off'].append((xor >> _log2(s)) == 1)
        s *= 2
    return masks


def _unit_lower_inverse(n_mats, masks):
    power = [jnp.where(masks['blk'], x, 0.0) for x in n_mats]
    inv = [masks['eye'] + x for x in power]
    width = 1
    blk_rows = min(INV_BLOCK, n_mats[0].shape[0])
    while 2 * width < blk_rows:
        power = [_dot(x, x) for x in power]
        inv = [i_ + _dot(i_, x) for i_, x in zip(inv, power)]
        width *= 2
    for off_mask in masks['off']:
        off = [jnp.where(off_mask, x, 0.0) for x in n_mats]
        tmp = [_dot(o_, i_) for o_, i_ in zip(off, inv)]
        inv = [i_ + _dot(i_, t_) for i_, t_ in zip(inv, tmp)]
    return inv


def _row_tiling(x, lead):
    seq = x.shape[1]
    tm = min(ROW_TILE, seq) if lead is None else lead.shape[0]
    assert seq % tm == 0
    n_lead = 0 if lead is None else 1
    return tm, seq // tm + n_lead, n_lead


def _prenorm_kernel(*refs, has_lead):
    x_ref, g_ref, o_ref = refs[0], refs[-2], refs[-1]
    x = x_ref[...]
    if has_lead:
        x = jnp.where(pl.program_id(1) == 0, refs[1][...], x)
    ms = jnp.mean(x * x, axis=-1, keepdims=True)
    o_ref[...] = (x * lax.rsqrt(ms + NORM_EPS) * g_ref[...]).astype(o_ref.dtype)


def _prenorm(x, lead, gain):
    batch = x.shape[0]
    tm, nb, n_lead = _row_tiling(x, lead)
    x_spec = pl.BlockSpec((None, tm, D_MODEL), lambda b, i: (b, jnp.maximum(i - n_lead, 0), 0))
    lead_specs = [pl.BlockSpec((tm, D_MODEL), lambda b, i: (0, 0))] if n_lead else []
    return pl.pallas_call(
        functools.partial(_prenorm_kernel, has_lead=bool(n_lead)),
        grid=(batch, nb),
        in_specs=[x_spec] + lead_specs + [pl.BlockSpec((1, D_MODEL), lambda b, i: (0, 0))],
        out_specs=pl.BlockSpec((tm, D_MODEL), lambda b, i: (b * nb + i, 0)),
        out_shape=jax.ShapeDtypeStruct((batch * nb * tm, D_MODEL), BF16),
        compiler_params=pltpu.CompilerParams(dimension_semantics=("arbitrary", "arbitrary"),
                                             vmem_limit_bytes=VMEM_LIMIT),
    )(x, *([lead] if n_lead else []), gain.reshape(1, D_MODEL))


def _proj_kernel(x_ref, w_ref, o_ref):
    o_ref[...] = jnp.dot(x_ref[...], w_ref[...], preferred_element_type=F32).astype(o_ref.dtype)


def _proj(xn, w):
    n = xn.shape[0]
    width = w.shape[1]
    tm = min(ROW_TILE, n)
    return pl.pallas_call(
        _proj_kernel,
        grid=(pl.cdiv(n, tm),),
        in_specs=[pl.BlockSpec((tm, D_MODEL), lambda i: (i, 0)),
                  pl.BlockSpec((D_MODEL, width), lambda i: (0, 0))],
        out_specs=pl.BlockSpec((tm, width), lambda i: (i, 0)),
        out_shape=jax.ShapeDtypeStruct((n, width), F32),
        compiler_params=pltpu.CompilerParams(dimension_semantics=("arbitrary",),
                                             vmem_limit_bytes=VMEM_LIMIT),
    )(xn, w)


def _rwkv_kernel(x_ref, hist0_ref, s0_ref, mu_ref, w0_ref, a0_ref, kk_ref, ka_ref, rk_ref, lnw_ref,
                 lnb_ref, w2a2_ref, tril_ref,
                 y_ref, s_ref, hist, *, chunk, n_sub):
    L = chunk
    rows = L * n_sub
    c = pl.program_id(1)

    @pl.when(c == 0)
    def _():
        hist[...] = hist0_ref[...]
        s_ref[...] = s0_ref[...]

    x = x_ref[...]
    ps = x[:, :RW_SHIFT_WIDTH]
    gate = x[:, RW_SHIFT_WIDTH:]
    full = jnp.concatenate([hist[...], ps], axis=0)
    prev = pltpu.roll(full, 1, axis=0)[SUBLANES:]
    hist[...] = ps[rows - SUBLANES:]
    xs = ps + mu_ref[...] * (prev - ps)
    r = xs[:, :RW_WIDTH]
    k = xs[:, RW_WIDTH:2 * RW_WIDTH]
    v = xs[:, 2 * RW_WIDTH:3 * RW_WIDTH]
    z = xs[:, 3 * RW_WIDTH:]
    lane = lax.broadcasted_iota(jnp.int32, (rows, LANES), 1)
    lhs = jnp.where(lane < RW_LORA, jnp.tanh(z), z)
    wa = _dot(lhs, w2a2_ref[...])
    w_log = -_softplus(-(w0_ref[...] + wa[:, :RW_WIDTH])) - 0.5
    logd = -jnp.exp(w_log)
    a = _sigmoid(a0_ref[...] + wa[:, RW_WIDTH:])
    kkr = k * kk_ref[...]
    kk = kkr * lax.rsqrt(_group_sum(kkr * kkr, RW_HEAD_DIM) + 1e-6)
    k2 = k * (1.0 + (a - 1.0) * ka_ref[...])
    bv = kk * a
    lc = _sel_l(tril_ref[...], logd)
    e_pos = jnp.exp(lc)
    e_neg = jnp.exp(-lc)
    e_last = [jnp.exp(lc[(s + 1) * L - 1:(s + 1) * L]) for s in range(n_sub)]
    e_last_rows = jnp.concatenate([jnp.broadcast_to(e, (L, RW_WIDTH)) for e in e_last], axis=0)
    at = -kk * jnp.exp(lc - logd)
    rt = r * e_pos
    bt = bv * e_neg
    kt = k2 * e_neg
    bh = bt * e_last_rows
    kh = kt * e_last_rows

    n = 2 * L
    masks = _tri_masks(n, L)
    first = lax.broadcasted_iota(jnp.int32, (L, LANES), 1) < RW_HEAD_DIM

    def stack(xp):
        return jnp.concatenate([jnp.where(first, xp, 0.0).astype(BF16),
                                jnp.where(first, 0.0, xp).astype(BF16)], axis=0)

    items = [(slice(s * L, (s + 1) * L), slice(p * LANES, (p + 1) * LANES))
             for s in range(n_sub) for p in range(N_PAIRS)]
    v_st = [stack(v[rs, sl]) for rs, sl in items]
    ar = [jnp.concatenate([stack(at[rs, sl]), stack(rt[rs, sl])], axis=0) for rs, sl in items]
    bk = [jnp.concatenate([stack(bt[rs, sl]), stack(kt[rs, sl])], axis=0) for rs, sl in items]
    bkh = [jnp.concatenate([stack(bh[rs, sl]), stack(kh[rs, sl])], axis=0) for rs, sl in items]
    m = [_dot_nt(x_, y_) for x_, y_ in zip(ar, bk)]
    n_ab = [jnp.where(masks['strict'], x_[:n, :n], 0.0) for x_ in m]
    t_inv = _unit_lower_inverse(n_ab, masks)
    akv = [_dot(jnp.where(masks['strict'], x_[:n, n:], 0.0), v_) for x_, v_ in zip(m, v_st)]
    a_r = [jnp.concatenate([jnp.where(masks['incl'], x_[n:, :n], 0.0).astype(BF16),
                            jnp.where(masks['incl'], x_[n:, n:], 0.0).astype(BF16)], axis=1) for x_ in m]
    state = [s_ref[p] for p in range(N_PAIRS)]
    y_rows = []
    for s in range(n_sub):
        it = range(s * N_PAIRS, (s + 1) * N_PAIRS)
        ars = [_dot_nt(ar[i], s_) for i, s_ in zip(it, state)]
        u = [_dot(t_inv[i], x_[:n] + akv[i]) for i, x_ in zip(it, ars)]
        uv = [jnp.concatenate([u_.astype(BF16), v_st[i]], axis=0) for i, u_ in zip(it, u)]
        y_st = [x_[n:] + _dot(a_r[i], uv_) for i, x_, uv_ in zip(it, ars, uv)]
        y_rows.append(jnp.concatenate([x_[:L] + x_[L:] for x_ in y_st], axis=1))
        state = [s_ * e_last[s][:, items[i][1]] + _dot_tn(uv_, bkh[i]) for i, s_, uv_ in zip(it, state, uv)]
    for p in range(N_PAIRS):
        s_ref[p] = state[p]

    y = jnp.concatenate(y_rows, axis=0)
    inv_dh = 1.0 / RW_HEAD_DIM
    mean = _group_sum(y, RW_HEAD_DIM) * inv_dh
    yc = y - mean
    var = _group_sum(yc * yc, RW_HEAD_DIM) * inv_dh
    yn = yc * lax.rsqrt(var + RW_GN_EPS) * lnw_ref[...] + lnb_ref[...]
    bonus = _group_sum(r * k2 * rk_ref[...], RW_HEAD_DIM) * v
    y_ref[...] = ((yn + bonus) * _silu(gate)).astype(y_ref.dtype)


def _rwkv(proj_rw, hist0, s0_bd, consts, batch, n_chunks, chunk, n_sub):
    L = chunk * n_sub
    n_chunks = n_chunks // n_sub
    rows = batch * n_chunks * L
    row = lambda w: pl.BlockSpec((1, w), lambda b, c: (0, 0))
    full = lambda a: pl.BlockSpec(a.shape, lambda b, c: (0,) * a.ndim)
    return pl.pallas_call(
        functools.partial(_rwkv_kernel, chunk=chunk, n_sub=n_sub),
        grid=(batch, n_chunks),
        in_specs=[pl.BlockSpec((L, SEC_RW), lambda b, c: (b * n_chunks + c, 0)),
                  pl.BlockSpec((None, SUBLANES, RW_SHIFT_WIDTH), lambda b, c: (b, 0, 0)),
                  pl.BlockSpec((None, N_PAIRS, LANES, LANES), lambda b, c: (b, 0, 0, 0)),
                  row(RW_SHIFT_WIDTH)] + [row(RW_WIDTH)] * 7 +
                 [full(consts['w2a2']), full(consts['tril'])],
        out_specs=[pl.BlockSpec((L, RW_WIDTH), lambda b, c: (b * n_chunks + c, 0)),
                   pl.BlockSpec((None, N_PAIRS, LANES, LANES), lambda b, c: (b, 0, 0, 0))],
        out_shape=[jax.ShapeDtypeStruct((rows, RW_WIDTH), BF16),
                   jax.ShapeDtypeStruct((batch, N_PAIRS, LANES, LANES), F32)],
        scratch_shapes=[pltpu.VMEM((SUBLANES, RW_SHIFT_WIDTH), F32)],
        compiler_params=pltpu.CompilerParams(dimension_semantics=("arbitrary", "arbitrary"),
                                             vmem_limit_bytes=VMEM_LIMIT),
    )(proj_rw, hist0, s0_bd, consts['mu'], consts['w0'], consts['a0'], consts['k_k'], consts['k_a'],
      consts['r_k'], consts['ln_w'], consts['ln_b'], consts['w2a2'], consts['tril'])


def _gdn_kernel(x_ref, hist0_ref, s0_ref, cw_ref, alog_ref, dtb_ref, gnw_ref, eal_ref, tril_ref, triu_ref,
                y_ref, s_ref, hist, *, chunk, n_sub):
    L = chunk
    rows = L * n_sub
    c = pl.program_id(1)

    @pl.when(c == 0)
    def _():
        hist[...] = hist0_ref[...]
        s_ref[...] = s0_ref[...]

    x = x_ref[...]
    cin = x[:, :GD_CONV_WIDTH]
    ba = x[:, GD_CONV_WIDTH:GD_CONV_WIDTH + LANES]
    gate = x[:, GD_CONV_WIDTH + LANES:]
    full = jnp.concatenate([hist[...], cin], axis=0)
    hist[...] = cin[rows - SUBLANES:]
    cw = cw_ref[...]
    conv = cw[GD_CONV - 1:GD_CONV] * cin
    for d in range(1, GD_CONV):
        conv = conv + cw[GD_CONV - 1 - d:GD_CONV - d] * pltpu.roll(full, d, axis=0)[SUBLANES:]
    act = _silu(conv)
    q = act[:, :GD_WIDTH]
    k = act[:, GD_WIDTH:2 * GD_WIDTH]
    v = act[:, 2 * GD_WIDTH:]
    q = q * lax.rsqrt(_group_sum(q * q, GD_HEAD_DIM) + 1e-6) * (GD_HEAD_DIM ** -0.5)
    k = k * lax.rsqrt(_group_sum(k * k, GD_HEAD_DIM) + 1e-6)
    beta = _sigmoid(ba)
    g = -jnp.exp(alog_ref[...]) * _softplus(ba + dtb_ref[...])
    lane_bcast = lambda t, j: jnp.broadcast_to(t[:, j:j + 1], (rows, LANES))
    beta_x = jnp.concatenate([lane_bcast(beta, h) for h in range(GD_HEADS)], axis=1)
    g_x = jnp.concatenate([lane_bcast(g, GD_HEADS + h) for h in range(GD_HEADS)], axis=1)
    gcol = _sel_l(tril_ref[...], g_x)
    g_l = _sel_r(g, eal_ref[...])
    grow = [_sel_tn(g_l[s * L:(s + 1) * L], triu_ref[...]) for s in range(n_sub)]
    masks = _tri_masks(L, L)
    eg_all = jnp.exp(gcol)

    items = [(s, h, slice(s * L, (s + 1) * L), slice(h * LANES, (h + 1) * LANES))
             for s in range(n_sub) for h in range(GD_HEADS)]
    kq = [jnp.concatenate([k[rs, sl].astype(BF16), q[rs, sl].astype(BF16)], axis=0) for _, _, rs, sl in items]
    kkqk = [_dot_nt(x_, k[rs, sl]) for x_, (_, _, rs, sl) in zip(kq, items)]
    dec = [jnp.where(masks['incl'], jnp.exp(gcol[rs, sl][:, :L] - grow[s][h * L:(h + 1) * L]), 0.0)
           for s, h, rs, sl in items]
    m = [jnp.where(masks['strict'], -(beta_x[rs, sl][:, :L] * d_ * x_[:L]), 0.0)
         for d_, x_, (_, _, rs, sl) in zip(dec, kkqk, items)]
    t_inv = _unit_lower_inverse(m, masks)
    qkd = [(x_[L:] * d_).astype(BF16) for x_, d_ in zip(kkqk, dec)]
    state = [s_ref[h] for h in range(GD_HEADS)]
    o_rows = []
    for s in range(n_sub):
        it = range(s * GD_HEADS, (s + 1) * GD_HEADS)
        kqs = [_dot(kq[i], s_) for i, s_ in zip(it, state)]
        delta = []
        for i, x_ in zip(it, kqs):
            _, _, rs, sl = items[i]
            delta.append(_dot(t_inv[i], beta_x[rs, sl] * (v[rs, sl] - eg_all[rs, sl] * x_[:L])))
        o_rows.append(jnp.concatenate(
            [eg_all[items[i][2], items[i][3]] * x_[L:] + _dot(qkd[i], dl) for i, x_, dl in zip(it, kqs, delta)],
            axis=1))
        new_state = []
        for i, s_, dl in zip(it, state, delta):
            _, _, rs, sl = items[i]
            g_c = gcol[rs, sl]
            g_last = g_c[L - 1:L]
            new_state.append(jnp.exp(g_last) * s_ + _dot_tn(k[rs, sl] * jnp.exp(g_last - g_c), dl))
        state = new_state
    for h in range(GD_HEADS):
        s_ref[h] = state[h]

    o = jnp.concatenate(o_rows, axis=0)
    ms = _group_sum(o * o, GD_HEAD_DIM) * (1.0 / GD_HEAD_DIM)
    on = o * lax.rsqrt(ms + NORM_EPS) * gnw_ref[...]
    y_ref[...] = (on * _silu(gate)).astype(y_ref.dtype)


def _gdn(proj_gd, hist0, s0, consts, batch, n_chunks, chunk, n_sub):
    L = chunk * n_sub
    n_chunks = n_chunks // n_sub
    rows = batch * n_chunks * L
    full = lambda a: pl.BlockSpec(a.shape, lambda b, c: (0,) * a.ndim)
    names = ['conv_w', 'a_log', 'dt_bias', 'gn_w', 'e_alpha_l', 'tril', 'triu']
    return pl.pallas_call(
        functools.partial(_gdn_kernel, chunk=chunk, n_sub=n_sub),
        grid=(batch, n_chunks),
        in_specs=[pl.BlockSpec((L, SEC_GD), lambda b, c: (b * n_chunks + c, 0)),
                  pl.BlockSpec((None, SUBLANES, GD_CONV_WIDTH), lambda b, c: (b, 0, 0)),
                  pl.BlockSpec((None, GD_HEADS, LANES, LANES), lambda b, c: (b, 0, 0, 0))] +
                 [full(consts[nm]) for nm in names],
        out_specs=[pl.BlockSpec((L, GD_WIDTH), lambda b, c: (b * n_chunks + c, 0)),
                   pl.BlockSpec((None, GD_HEADS, LANES, LANES), lambda b, c: (b, 0, 0, 0))],
        out_shape=[jax.ShapeDtypeStruct((rows, GD_WIDTH), BF16),
                   jax.ShapeDtypeStruct((batch, GD_HEADS, LANES, LANES), F32)],
        scratch_shapes=[pltpu.VMEM((SUBLANES, GD_CONV_WIDTH), F32)],
        compiler_params=pltpu.CompilerParams(dimension_semantics=("arbitrary", "arbitrary"),
                                             vmem_limit_bytes=VMEM_LIMIT),
    )(proj_gd, hist0, s0, *[consts[nm] for nm in names])


def _out_kernel(h_ref, yrw_ref, ygd_ref, mg_ref, woa_ref, wob_ref, wo_ref, g_ref, o_ref, *, n_lead):
    @pl.when(pl.program_id(1) >= n_lead)
    def _():
        mg = mg_ref[...]
        ya = jnp.dot(yrw_ref[...], woa_ref[...], preferred_element_type=F32)
        yb = jnp.dot(ygd_ref[...], wob_ref[...], preferred_element_type=F32)
        merged = _sigmoid(mg[:, :D_MODEL]) * ya + _sigmoid(mg[:, D_MODEL:]) * yb
        out = _dot(merged, wo_ref[...])
        ms = jnp.mean(out * out, axis=-1, keepdims=True)
        o_ref[...] = h_ref[...] + out * lax.rsqrt(ms + NORM_EPS) * g_ref[...]


def _merge_out(x, lead, y_rw, y_gd, proj_mg, consts):
    batch, seq, _ = x.shape
    tm, nb, n_lead = _row_tiling(x, lead)
    x_spec = pl.BlockSpec((None, tm, D_MODEL), lambda b, i: (b, jnp.maximum(i - n_lead, 0), 0))
    rows = lambda w: pl.BlockSpec((tm, w), lambda b, i: (b * nb + i, 0))
    wspec = pl.BlockSpec((D_MODEL, D_MODEL), lambda b, i: (0, 0))
    return pl.pallas_call(
        functools.partial(_out_kernel, n_lead=n_lead),
        grid=(batch, nb),
        in_specs=[x_spec, rows(RW_WIDTH), rows(GD_WIDTH), rows(SEC_MG), wspec, wspec, wspec,
                  pl.BlockSpec((1, D_MODEL), lambda b, i: (0, 0))],
        out_specs=x_spec,
        out_shape=jax.ShapeDtypeStruct((batch, seq, D_MODEL), F32),
        compiler_params=pltpu.CompilerParams(dimension_semantics=("arbitrary", "arbitrary"),
                                             vmem_limit_bytes=VMEM_LIMIT),
    )(x, y_rw, y_gd, proj_mg, consts['w_oa'], consts['w_ob'], consts['w_o'], consts['norm_post'])


def _chunk_consts(chunk, n_sub):
    L = chunk
    tril = np.tril(np.ones((L, L), np.float32))
    ea_l = np.zeros((LANES, GD_HEADS * L), np.float32)
    for h in range(GD_HEADS):
        ea_l[GD_HEADS + h, h * L:(h + 1) * L] = 1.0
    return {'tril': jnp.asarray(np.kron(np.eye(n_sub, dtype=np.float32), tril), BF16),
            'triu': jnp.asarray(tril.T, BF16), 'e_alpha_l': jnp.asarray(ea_l, BF16)}


def _layer_consts(norm_post, w_in, rw_mu, rw_w0, rw_w2, rw_a0, rw_a2, rw_k_k, rw_k_a, rw_r_k, rw_ln_w,
                  rw_ln_b, gd_conv_w, gd_a_log, gd_dt_bias, gd_norm_w, w_out_a, w_out_b, w_out):
    off_gate = RW_SHIFT_WIDTH
    off_conv = off_gate + RW_WIDTH
    off_beta = off_conv + GD_CONV_WIDTH
    off_ggate = off_beta + 2 * GD_HEADS
    off_merge = off_ggate + GD_WIDTH
    w_bf = w_in.astype(BF16)
    pad = jnp.zeros((D_MODEL, LANES - 2 * GD_HEADS), BF16)
    w2a2 = jnp.zeros((2 * RW_LORA, 2 * RW_WIDTH), F32)
    w2a2 = w2a2.at[:RW_LORA, :RW_WIDTH].set(rw_w2).at[RW_LORA:, RW_WIDTH:].set(rw_a2)
    lane_row = lambda vec: jnp.zeros((1, LANES), F32).at[0, GD_HEADS:2 * GD_HEADS].set(vec)
    row = lambda a: a.reshape(1, -1).astype(F32)
    return {
        'w_rw': w_bf[:, :off_conv],
        'w_gd': jnp.concatenate([w_bf[:, off_conv:off_ggate], pad, w_bf[:, off_ggate:off_merge]], axis=1),
        'w_mg': w_bf[:, off_merge:],
        'mu': row(rw_mu), 'w0': row(rw_w0), 'a0': row(rw_a0), 'k_k': row(rw_k_k), 'k_a': row(rw_k_a),
        'r_k': row(rw_r_k), 'ln_w': row(rw_ln_w), 'ln_b': row(rw_ln_b),
        'w2a2': w2a2.astype(BF16),
        'conv_w': gd_conv_w.astype(F32), 'a_log': lane_row(gd_a_log), 'dt_bias': lane_row(gd_dt_bias),
        'gn_w': row(jnp.tile(gd_norm_w, GD_HEADS)),
        'w_oa': w_out_a.astype(BF16), 'w_ob': w_out_b.astype(BF16), 'w_o': w_out.astype(BF16),
        'norm_post': row(norm_post),
    }


def _pair_block_diag(s):
    b = s.shape[0]
    s5 = s.reshape(b, N_PAIRS, 2, RW_HEAD_DIM, RW_HEAD_DIM)
    bd = jnp.einsum('bphvk,hg->bphvgk', s5, jnp.eye(2, dtype=s.dtype))
    return bd.reshape(b, N_PAIRS, LANES, LANES)


def _pair_diag_blocks(bd):
    b = bd.shape[0]
    bd6 = bd.reshape(b, N_PAIRS, 2, RW_HEAD_DIM, 2, RW_HEAD_DIM)
    s5 = jnp.einsum('bphvgk,hg->bphvk', bd6, jnp.eye(2, dtype=bd.dtype))
    return s5.reshape(b, RW_HEADS, RW_HEAD_DIM, RW_HEAD_DIM)


def _trunk(x, lead, streams, chunk, n_sub, shift_prev, conv_prev, s_rw, s_gd, norm_pre, consts):
    total = x.shape[0] * (x.shape[1] + (0 if lead is None else lead.shape[0]))
    t = total // streams
    n_chunks = t // chunk
    cc = dict(consts, **_chunk_consts(chunk, n_sub))
    xn = _prenorm(x, lead, norm_pre)
    proj_rw = _proj(xn, consts['w_rw'])
    proj_gd = _proj(xn, consts['w_gd'])
    proj_mg = _proj(xn, consts['w_mg'])
    hist_rw = jnp.zeros((streams, SUBLANES, RW_SHIFT_WIDTH), F32).at[:, SUBLANES - 1].set(shift_prev)
    hist_gd = jnp.zeros((streams, SUBLANES, GD_CONV_WIDTH), F32).at[:, SUBLANES - (GD_CONV - 1):].set(conv_prev)
    y_rw, s_rw_bd = _rwkv(proj_rw, hist_rw, _pair_block_diag(s_rw), cc, streams, n_chunks, chunk, n_sub)
    y_gd, s_gd_new = _gdn(proj_gd, hist_gd, s_gd, cc, streams, n_chunks, chunk, n_sub)
    x_new = _merge_out(x, lead, y_rw, y_gd, proj_mg, consts)
    p_rw = proj_rw.reshape(streams, t, SEC_RW)
    p_gd = proj_gd.reshape(streams, t, SEC_GD)
    shift_new = p_rw[:, -1, :RW_SHIFT_WIDTH]
    conv_new = p_gd[:, t - (GD_CONV - 1):, :GD_CONV_WIDTH]
    return x_new, (shift_new, _pair_diag_blocks(s_rw_bd), conv_new, s_gd_new)


def kernel(x_prompt, x_sample, state_rwkv_shift, state_rwkv_wkv, state_gdn_conv, state_gdn_ssm, meta_tokens, norm_pre, w_in, rw_mu, rw_w0, rw_w2, rw_a0, rw_a2, rw_k_k, rw_k_a, rw_r_k, rw_ln_w, rw_ln_b, gd_conv_w, gd_a_log, gd_dt_bias, gd_norm_w, w_out_a, w_out_b, w_out, norm_post):
    assert w_in.shape[0] == 1, "single layer: the meta rows' outputs are not carried to a next layer"
    l = 0
    bp, seq, _ = x_prompt.shape
    bs, dec_seq, _ = x_sample.shape
    n_front = (-(N_META + seq)) % (PROMPT_CHUNK * PROMPT_SUB)
    lead = jnp.concatenate([jnp.zeros((n_front, D_MODEL), F32), meta_tokens.astype(F32)], axis=0)
    consts = _layer_consts(norm_post[l], w_in[l], rw_mu[l], rw_w0[l], rw_w2[l], rw_a0[l], rw_a2[l],
                           rw_k_k[l], rw_k_a[l], rw_r_k[l], rw_ln_w[l], rw_ln_b[l], gd_conv_w[l],
                           gd_a_log[l], gd_dt_bias[l], gd_norm_w[l], w_out_a[l], w_out_b[l], w_out[l])
    y_prompt, new_p = _trunk(x_prompt, lead, bp, PROMPT_CHUNK, PROMPT_SUB,
                             jnp.zeros((bp, RW_SHIFT_WIDTH), F32),
                             jnp.zeros((bp, GD_CONV - 1, GD_CONV_WIDTH), F32),
                             jnp.zeros((bp, RW_HEADS, RW_HEAD_DIM, RW_HEAD_DIM), F32),
                             jnp.zeros((bp, GD_HEADS, GD_HEAD_DIM, GD_HEAD_DIM), F32),
                             norm_pre[l], consts)
    ys, new_s = _trunk(x_sample.reshape(1, bs * dec_seq, D_MODEL), None, bs, dec_seq, 1,
                       state_rwkv_shift[l], state_gdn_conv[l], state_rwkv_wkv[l], state_gdn_ssm[l],
                       norm_pre[l], consts)
    y_sample = ys.reshape(bs, dec_seq, D_MODEL)
    return (y_prompt, y_sample) + tuple(t[None] for t in new_p) + tuple(t[None] for t in new_s)
```

```python
import functools

import numpy as np
import jax
import jax.numpy as jnp
from jax import lax
from jax.experimental import pallas as pl
from jax.experimental.pallas import tpu as pltpu

F32 = jnp.float32
BF16 = jnp.bfloat16

D_MODEL = 1024
N_META = 16
NORM_EPS = 1e-6
RW_HEADS = 16
RW_HEAD_DIM = 64
RW_WIDTH = 1024
RW_LORA = 64
RW_SHIFT_WIDTH = 3 * RW_WIDTH + 2 * RW_LORA
RW_GN_EPS = 64e-5
GD_HEADS = 8
GD_HEAD_DIM = 128
GD_WIDTH = 1024
GD_CONV = 4
GD_CONV_WIDTH = 3 * GD_WIDTH
LANES = 128
SUBLANES = 8
N_PAIRS = RW_HEADS // 2
SEC_RW = RW_SHIFT_WIDTH + RW_WIDTH
SEC_GD = GD_CONV_WIDTH + LANES + GD_WIDTH
SEC_MG = 2 * D_MODEL
PROMPT_CHUNK = 64
INV_BLOCK = 16
PROMPT_SUB = 4
VMEM_LIMIT = 56 * 1024 * 1024
ROW_TILE = 512


def _dot(a, b):
    return jnp.dot(a.astype(BF16), b.astype(BF16), preferred_element_type=F32)


def _dot_nt(a, b):
    return lax.dot_general(a.astype(BF16), b.astype(BF16), (((1,), (1,)), ((), ())),
                           preferred_element_type=F32)


def _dot_tn(a, b):
    return lax.dot_general(a.astype(BF16), b.astype(BF16), (((0,), (0,)), ((), ())),
                           preferred_element_type=F32)


def _split3(x):
    hi = x.astype(BF16)
    r1 = x - hi.astype(F32)
    mid = r1.astype(BF16)
    lo = (r1 - mid.astype(F32)).astype(BF16)
    return hi, mid, lo


def _sel_r(x, m01):
    hi, mid, lo = _split3(x)
    d = lambda p: jnp.dot(p, m01, preferred_element_type=F32)
    return d(hi) + d(mid) + d(lo)


def _sel_l(m01, x):
    hi, mid, lo = _split3(x)
    d = lambda p: jnp.dot(m01, p, preferred_element_type=F32)
    return d(hi) + d(mid) + d(lo)


def _sel_tn(x, m01):
    hi, mid, lo = _split3(x)
    d = lambda p: lax.dot_general(p, m01, (((0,), (0,)), ((), ())), preferred_element_type=F32)
    return d(hi) + d(mid) + d(lo)


def _sigmoid(x):
    return 1.0 / (1.0 + jnp.exp(-x))


def _silu(x):
    return x * _sigmoid(x)


def _softplus(x):
    return jnp.maximum(x, 0.0) + jnp.log(1.0 + jnp.exp(-jnp.abs(x)))


def _group_sum(x, group):
    parts = []
    for c in range(x.shape[1] // LANES):
        xs = x[:, c * LANES:(c + 1) * LANES]
        if group == LANES:
            parts.append(jnp.broadcast_to(jnp.sum(xs, axis=-1, keepdims=True), xs.shape))
        else:
            assert 2 * group == LANES
            first = lax.broadcasted_iota(jnp.int32, xs.shape, 1) < group
            s0 = jnp.sum(jnp.where(first, xs, 0.0), axis=-1, keepdims=True)
            s1 = jnp.sum(jnp.where(first, 0.0, xs), axis=-1, keepdims=True)
            parts.append(jnp.where(first, s0, s1))
    return jnp.concatenate(parts, axis=1)


def _log2(n):
    assert n & (n - 1) == 0
    return n.bit_length() - 1


def _tri_masks(n, chunk):
    ii = lax.broadcasted_iota(jnp.int32, (n, n), 0)
    jj = lax.broadcasted_iota(jnp.int32, (n, n), 1)
    ti = ii & (chunk - 1)
    tj = jj & (chunk - 1)
    xor = ii ^ jj
    blk = min(INV_BLOCK, chunk)
    masks = {
        'eye': (ii == jj).astype(F32),
        'strict': ti > tj,
        'incl': ti >= tj,
        'blk': (xor >> _log2(blk)) == 0,
        'off': [],
    }
    s = blk
    while s < chunk:
        masks['off'].append((xor >> _log2(s)) == 1)
        s *= 2
    return masks


def _unit_lower_inverse(n_mats, masks):
    power = [jnp.where(masks['blk'], x, 0.0) for x in n_mats]
    inv = [masks['eye'] + x for x in power]
    width = 1
    blk_rows = min(INV_BLOCK, n_mats[0].shape[0])
    while 2 * width < blk_rows:
        power = [_dot(x, x) for x in power]
        inv = [i_ + _dot(i_, x) for i_, x in zip(inv, power)]
        width *= 2
    for off_mask in masks['off']:
        off = [jnp.where(off_mask, x, 0.0) for x in n_mats]
        tmp = [_dot(o_, i_) for o_, i_ in zip(off, inv)]
        inv = [i_ + _dot(i_, t_) for i_, t_ in zip(inv, tmp)]
    return inv


def _row_tiling(x, lead):
    seq = x.shape[1]
    tm = min(ROW_TILE, seq) if lead is None else lead.shape[0]
    assert seq % tm == 0
    n_lead = 0 if lead is None else 1
    return tm, seq // tm + n_lead, n_lead


def _prenorm_kernel(*refs, has_lead):
    x_ref, g_ref, o_ref = refs[0], refs[-2], refs[-1]
    x = x_ref[...]
    if has_lead:
        x = jnp.where(pl.program_id(1) == 0, refs[1][...], x)
    ms = jnp.mean(x * x, axis=-1, keepdims=True)
    o_ref[...] = (x * lax.rsqrt(ms + NORM_EPS) * g_ref[...]).astype(o_ref.dtype)


def _prenorm(x, lead, gain):
    batch = x.shape[0]
    tm, nb, n_lead = _row_tiling(x, lead)
    x_spec = pl.BlockSpec((None, tm, D_MODEL), lambda b, i: (b, jnp.maximum(i - n_lead, 0), 0))
    lead_specs = [pl.BlockSpec((tm, D_MODEL), lambda b, i: (0, 0))] if n_lead else []
    return pl.pallas_call(
        functools.partial(_prenorm_kernel, has_lead=bool(n_lead)),
        grid=(batch, nb),
        in_specs=[x_spec] + lead_specs + [pl.BlockSpec((1, D_MODEL), lambda b, i: (0, 0))],
        out_specs=pl.BlockSpec((tm, D_MODEL), lambda b, i: (b * nb + i, 0)),
        out_shape=jax.ShapeDtypeStruct((batch * nb * tm, D_MODEL), BF16),
        compiler_params=pltpu.CompilerParams(dimension_semantics=("arbitrary", "arbitrary"),
                                             vmem_limit_bytes=VMEM_LIMIT),
    )(x, *([lead] if n_lead else []), gain.reshape(1, D_MODEL))


def _proj_kernel(x_ref, w_ref, o_ref):
    o_ref[...] = jnp.dot(x_ref[...], w_ref[...], preferred_element_type=F32).astype(o_ref.dtype)


def _proj(xn, w):
    n = xn.shape[0]
    width = w.shape[1]
    tm = min(ROW_TILE, n)
    return pl.pallas_call(
        _proj_kernel,
        grid=(pl.cdiv(n, tm),),
        in_specs=[pl.BlockSpec((tm, D_MODEL), lambda i: (i, 0)),
                  pl.BlockSpec((D_MODEL, width), lambda i: (0, 0))],
        out_specs=pl.BlockSpec((tm, width), lambda i: (i, 0)),
        out_shape=jax.ShapeDtypeStruct((n, width), F32),
        compiler_params=pltpu.CompilerParams(dimension_semantics=("arbitrary",),
                                             vmem_limit_bytes=VMEM_LIMIT),
    )(xn, w)


N_RW_IN = 13
N_GD_IN = 10


def _recur_kernel(*refs, chunk, n_sub):
    rw_in, gd_in = refs[:N_RW_IN], refs[N_RW_IN:N_RW_IN + N_GD_IN]
    y_rw, s_rw, y_gd, s_gd, hist_rw, hist_gd = refs[N_RW_IN + N_GD_IN:]

    @pl.when(pl.program_id(1) == 0)
    def _():
        hist_rw[...] = rw_in[1][...]
        s_rw[...] = rw_in[2][...]
        hist_gd[...] = gd_in[1][...]
        s_gd[...] = gd_in[2][...]

    phases = [_rwkv_phases(rw_in[0], *rw_in[3:], y_rw, s_rw, hist_rw, chunk=chunk, n_sub=n_sub),
              _gdn_phases(gd_in[0], *gd_in[3:], y_gd, s_gd, hist_gd, chunk=chunk, n_sub=n_sub)]
    while phases:
        for ph in list(phases):
            if next(ph, 'done') == 'done':
                phases.remove(ph)


def _rwkv_phases(x_ref, mu_ref, w0_ref, a0_ref, kk_ref, ka_ref, rk_ref, lnw_ref, lnb_ref, w2a2_ref, tril_ref,
                 y_ref, s_ref, hist, *, chunk, n_sub):
    L = chunk
    rows = L * n_sub
    x = x_ref[...]
    ps = x[:, :RW_SHIFT_WIDTH]
    gate = x[:, RW_SHIFT_WIDTH:]
    full = jnp.concatenate([hist[...], ps], axis=0)
    prev = pltpu.roll(full, 1, axis=0)[SUBLANES:]
    hist[...] = ps[rows - SUBLANES:]
    xs = ps + mu_ref[...] * (prev - ps)
    r = xs[:, :RW_WIDTH]
    k = xs[:, RW_WIDTH:2 * RW_WIDTH]
    v = xs[:, 2 * RW_WIDTH:3 * RW_WIDTH]
    z = xs[:, 3 * RW_WIDTH:]
    lane = lax.broadcasted_iota(jnp.int32, (rows, LANES), 1)
    lhs = jnp.where(lane < RW_LORA, jnp.tanh(z), z)
    wa = _dot(lhs, w2a2_ref[...])
    w_log = -_softplus(-(w0_ref[...] + wa[:, :RW_WIDTH])) - 0.5
    logd = -jnp.exp(w_log)
    a = _sigmoid(a0_ref[...] + wa[:, RW_WIDTH:])
    kkr = k * kk_ref[...]
    kk = kkr * lax.rsqrt(_group_sum(kkr * kkr, RW_HEAD_DIM) + 1e-6)
    k2 = k * (1.0 + (a - 1.0) * ka_ref[...])
    bv = kk * a
    lc = _sel_l(tril_ref[...], logd)
    e_pos = jnp.exp(lc)
    e_neg = jnp.exp(-lc)
    e_last = [jnp.exp(lc[(s + 1) * L - 1:(s + 1) * L]) for s in range(n_sub)]
    e_last_rows = jnp.concatenate([jnp.broadcast_to(e, (L, RW_WIDTH)) for e in e_last], axis=0)
    at = -kk * jnp.exp(lc - logd)
    rt = r * e_pos
    bt = bv * e_neg
    kt = k2 * e_neg
    bh = bt * e_last_rows
    kh = kt * e_last_rows
    yield

    n = 2 * L
    masks = _tri_masks(n, L)
    first = lax.broadcasted_iota(jnp.int32, (L, LANES), 1) < RW_HEAD_DIM

    def stack(xp):
        return jnp.concatenate([jnp.where(first, xp, 0.0).astype(BF16),
                                jnp.where(first, 0.0, xp).astype(BF16)], axis=0)

    items = [(slice(s * L, (s + 1) * L), slice(p * LANES, (p + 1) * LANES))
             for s in range(n_sub) for p in range(N_PAIRS)]
    v_st = [stack(v[rs, sl]) for rs, sl in items]
    ar = [jnp.concatenate([stack(at[rs, sl]), stack(rt[rs, sl])], axis=0) for rs, sl in items]
    bk = [jnp.concatenate([stack(bt[rs, sl]), stack(kt[rs, sl])], axis=0) for rs, sl in items]
    bkh = [jnp.concatenate([stack(bh[rs, sl]), stack(kh[rs, sl])], axis=0) for rs, sl in items]
    m = [_dot_nt(x_, y_) for x_, y_ in zip(ar, bk)]
    n_ab = [jnp.where(masks['strict'], x_[:n, :n], 0.0) for x_ in m]
    t_inv = _unit_lower_inverse(n_ab, masks)
    akv = [_dot(jnp.where(masks['strict'], x_[:n, n:], 0.0), v_) for x_, v_ in zip(m, v_st)]
    a_r = [jnp.concatenate([jnp.where(masks['incl'], x_[n:, :n], 0.0).astype(BF16),
                            jnp.where(masks['incl'], x_[n:, n:], 0.0).astype(BF16)], axis=1) for x_ in m]
    yield
    state = [s_ref[p] for p in range(N_PAIRS)]
    y_rows = []
    for s in range(n_sub):
        it = range(s * N_PAIRS, (s + 1) * N_PAIRS)
        ars = [_dot_nt(ar[i], s_) for i, s_ in zip(it, state)]
        u = [_dot(t_inv[i], x_[:n] + akv[i]) for i, x_ in zip(it, ars)]
        uv = [jnp.concatenate([u_.astype(BF16), v_st[i]], axis=0) for i, u_ in zip(it, u)]
        y_st = [x_[n:] + _dot(a_r[i], uv_) for i, x_, uv_ in zip(it, ars, uv)]
        y_rows.append(jnp.concatenate([x_[:L] + x_[L:] for x_ in y_st], axis=1))
        state = [s_ * e_last[s][:, items[i][1]] + _dot_tn(uv_, bkh[i]) for i, s_, uv_ in zip(it, state, uv)]
        yield
    for p in range(N_PAIRS):
        s_ref[p] = state[p]

    y = jnp.concatenate(y_rows, axis=0)
    inv_dh = 1.0 / RW_HEAD_DIM
    mean = _group_sum(y, RW_HEAD_DIM) * inv_dh
    yc = y - mean
    var = _group_sum(yc * yc, RW_HEAD_DIM) * inv_dh
    yn = yc * lax.rsqrt(var + RW_GN_EPS) * lnw_ref[...] + lnb_ref[...]
    bonus = _group_sum(r * k2 * rk_ref[...], RW_HEAD_DIM) * v
    y_ref[...] = ((yn + bonus) * _silu(gate)).astype(y_ref.dtype)


def _gdn_phases(x_ref, cw_ref, alog_ref, dtb_ref, gnw_ref, eal_ref, tril_ref, triu_ref,
                y_ref, s_ref, hist, *, chunk, n_sub):
    L = chunk
    rows = L * n_sub
    x = x_ref[...]
    cin = x[:, :GD_CONV_WIDTH]
    ba = x[:, GD_CONV_WIDTH:GD_CONV_WIDTH + LANES]
    gate = x[:, GD_CONV_WIDTH + LANES:]
    full = jnp.concatenate([hist[...], cin], axis=0)
    hist[...] = cin[rows - SUBLANES:]
    cw = cw_ref[...]
    conv = cw[GD_CONV - 1:GD_CONV] * cin
    for d in range(1, GD_CONV):
        conv = conv + cw[GD_CONV - 1 - d:GD_CONV - d] * pltpu.roll(full, d, axis=0)[SUBLANES:]
    act = _silu(conv)
    q = act[:, :GD_WIDTH]
    k = act[:, GD_WIDTH:2 * GD_WIDTH]
    v = act[:, 2 * GD_WIDTH:]
    q = q * lax.rsqrt(_group_sum(q * q, GD_HEAD_DIM) + 1e-6) * (GD_HEAD_DIM ** -0.5)
    k = k * lax.rsqrt(_group_sum(k * k, GD_HEAD_DIM) + 1e-6)
    beta = _sigmoid(ba)
    g = -jnp.exp(alog_ref[...]) * _softplus(ba + dtb_ref[...])
    lane_bcast = lambda t, j: jnp.broadcast_to(t[:, j:j + 1], (rows, LANES))
    beta_x = jnp.concatenate([lane_bcast(beta, h) for h in range(GD_HEADS)], axis=1)
    g_x = jnp.concatenate([lane_bcast(g, GD_HEADS + h) for h in range(GD_HEADS)], axis=1)
    gcol = _sel_l(tril_ref[...], g_x)
    g_l = _sel_r(g, eal_ref[...])
    grow = [_sel_tn(g_l[s * L:(s + 1) * L], triu_ref[...]) for s in range(n_sub)]
    masks = _tri_masks(L, L)
    eg_all = jnp.exp(gcol)
    yield

    items = [(s, h, slice(s * L, (s + 1) * L), slice(h * LANES, (h + 1) * LANES))
             for s in range(n_sub) for h in range(GD_HEADS)]
    kq = [jnp.concatenate([k[rs, sl].astype(BF16), q[rs, sl].astype(BF16)], axis=0) for _, _, rs, sl in items]
    kkqk = [_dot_nt(x_, k[rs, sl]) for x_, (_, _, rs, sl) in zip(kq, items)]
    dec = [jnp.where(masks['incl'], jnp.exp(gcol[rs, sl][:, :L] - grow[s][h * L:(h + 1) * L]), 0.0)
           for s, h, rs, sl in items]
    m = [jnp.where(masks['strict'], -(beta_x[rs, sl][:, :L] * d_ * x_[:L]), 0.0)
         for d_, x_, (_, _, rs, sl) in zip(dec, kkqk, items)]
    t_inv = _unit_lower_inverse(m, masks)
    qkd = [(x_[L:] * d_).astype(BF16) for x_, d_ in zip(kkqk, dec)]
    yield
    state = [s_ref[h] for h in range(GD_HEADS)]
    o_rows = []
    for s in range(n_sub):
        it = range(s * GD_HEADS, (s + 1) * GD_HEADS)
        kqs = [_dot(kq[i], s_) for i, s_ in zip(it, state)]
        delta = []
        for i, x_ in zip(it, kqs):
            _, _, rs, sl = items[i]
            delta.append(_dot(t_inv[i], beta_x[rs, sl] * (v[rs, sl] - eg_all[rs, sl] * x_[:L])))
        o_rows.append(jnp.concatenate(
            [eg_all[items[i][2], items[i][3]] * x_[L:] + _dot(qkd[i], dl) for i, x_, dl in zip(it, kqs, delta)],
            axis=1))
        new_state = []
        for i, s_, dl in zip(it, state, delta):
            _, _, rs, sl = items[i]
            g_c = gcol[rs, sl]
            g_last = g_c[L - 1:L]
            new_state.append(jnp.exp(g_last) * s_ + _dot_tn(k[rs, sl] * jnp.exp(g_last - g_c), dl))
        state = new_state
        yield
    for h in range(GD_HEADS):
        s_ref[h] = state[h]

    o = jnp.concatenate(o_rows, axis=0)
    ms = _group_sum(o * o, GD_HEAD_DIM) * (1.0 / GD_HEAD_DIM)
    on = o * lax.rsqrt(ms + NORM_EPS) * gnw_ref[...]
    y_ref[...] = (on * _silu(gate)).astype(y_ref.dtype)


def _recurrences(proj_rw, hist_rw, s_rw_bd, proj_gd, hist_gd, s_gd, consts, batch, n_chunks, chunk, n_sub):
    L = chunk * n_sub
    n_steps = n_chunks // n_sub
    rows = batch * n_steps * L
    row = lambda w: pl.BlockSpec((1, w), lambda b, c: (0, 0))
    full = lambda a: pl.BlockSpec(a.shape, lambda b, c: (0,) * a.ndim)
    step_rows = lambda w: pl.BlockSpec((L, w), lambda b, c: (b * n_steps + c, 0))
    per_stream = lambda *shape: pl.BlockSpec((None,) + shape, lambda b, c: (b,) + (0,) * len(shape))
    rw_names = ['mu', 'w0', 'a0', 'k_k', 'k_a', 'r_k', 'ln_w', 'ln_b', 'w2a2', 'tril']
    gd_names = ['conv_w', 'a_log', 'dt_bias', 'gn_w', 'e_alpha_l', 'tril', 'triu']
    rw_specs = ([step_rows(SEC_RW), per_stream(SUBLANES, RW_SHIFT_WIDTH), per_stream(N_PAIRS, LANES, LANES),
                 row(RW_SHIFT_WIDTH)] + [row(RW_WIDTH)] * 7 + [full(consts['w2a2']), full(consts['tril'])])
    gd_specs = ([step_rows(SEC_GD), per_stream(SUBLANES, GD_CONV_WIDTH), per_stream(GD_HEADS, LANES, LANES)] +
                [full(consts[nm]) for nm in gd_names])
    assert len(rw_specs) == N_RW_IN and len(gd_specs) == N_GD_IN
    return pl.pallas_call(
        functools.partial(_recur_kernel, chunk=chunk, n_sub=n_sub),
        grid=(batch, n_steps),
        in_specs=rw_specs + gd_specs,
        out_specs=[step_rows(RW_WIDTH), per_stream(N_PAIRS, LANES, LANES),
                   step_rows(GD_WIDTH), per_stream(GD_HEADS, LANES, LANES)],
        out_shape=[jax.ShapeDtypeStruct((rows, RW_WIDTH), BF16),
                   jax.ShapeDtypeStruct((batch, N_PAIRS, LANES, LANES), F32),
                   jax.ShapeDtypeStruct((rows, GD_WIDTH), BF16),
                   jax.ShapeDtypeStruct((batch, GD_HEADS, LANES, LANES), F32)],
        scratch_shapes=[pltpu.VMEM((SUBLANES, RW_SHIFT_WIDTH), F32), pltpu.VMEM((SUBLANES, GD_CONV_WIDTH), F32)],
        compiler_params=pltpu.CompilerParams(dimension_semantics=("arbitrary", "arbitrary"),
                                             vmem_limit_bytes=VMEM_LIMIT),
    )(proj_rw, hist_rw, s_rw_bd, *[consts[nm] for nm in rw_names],
      proj_gd, hist_gd, s_gd, *[consts[nm] for nm in gd_names])


def _out_kernel(h_ref, yrw_ref, ygd_ref, mg_ref, woa_ref, wob_ref, wo_ref, g_ref, o_ref, *, n_lead):
    @pl.when(pl.program_id(1) >= n_lead)
    def _():
        mg = mg_ref[...]
        ya = jnp.dot(yrw_ref[...], woa_ref[...], preferred_element_type=F32)
        yb = jnp.dot(ygd_ref[...], wob_ref[...], preferred_element_type=F32)
        merged = _sigmoid(mg[:, :D_MODEL]) * ya + _sigmoid(mg[:, D_MODEL:]) * yb
        out = _dot(merged, wo_ref[...])
        ms = jnp.mean(out * out, axis=-1, keepdims=True)
        o_ref[...] = h_ref[...] + out * lax.rsqrt(ms + NORM_EPS) * g_ref[...]


def _merge_out(x, lead, y_rw, y_gd, proj_mg, consts):
    batch, seq, _ = x.shape
    tm, nb, n_lead = _row_tiling(x, lead)
    x_spec = pl.BlockSpec((None, tm, D_MODEL), lambda b, i: (b, jnp.maximum(i - n_lead, 0), 0))
    rows = lambda w: pl.BlockSpec((tm, w), lambda b, i: (b * nb + i, 0))
    wspec = pl.BlockSpec((D_MODEL, D_MODEL), lambda b, i: (0, 0))
    return pl.pallas_call(
        functools.partial(_out_kernel, n_lead=n_lead),
        grid=(batch, nb),
        in_specs=[x_spec, rows(RW_WIDTH), rows(GD_WIDTH), rows(SEC_MG), wspec, wspec, wspec,
                  pl.BlockSpec((1, D_MODEL), lambda b, i: (0, 0))],
        out_specs=x_spec,
        out_shape=jax.ShapeDtypeStruct((batch, seq, D_MODEL), F32),
        compiler_params=pltpu.CompilerParams(dimension_semantics=("arbitrary", "arbitrary"),
                                             vmem_limit_bytes=VMEM_LIMIT),
    )(x, y_rw, y_gd, proj_mg, consts['w_oa'], consts['w_ob'], consts['w_o'], consts['norm_post'])


def _chunk_consts(chunk, n_sub):
    L = chunk
    tril = np.tril(np.ones((L, L), np.float32))
    ea_l = np.zeros((LANES, GD_HEADS * L), np.float32)
    for h in range(GD_HEADS):
        ea_l[GD_HEADS + h, h * L:(h + 1) * L] = 1.0
    return {'tril': jnp.asarray(np.kron(np.eye(n_sub, dtype=np.float32), tril), BF16),
            'triu': jnp.asarray(tril.T, BF16), 'e_alpha_l': jnp.asarray(ea_l, BF16)}


def _layer_consts(norm_post, w_in, rw_mu, rw_w0, rw_w2, rw_a0, rw_a2, rw_k_k, rw_k_a, rw_r_k, rw_ln_w,
                  rw_ln_b, gd_conv_w, gd_a_log, gd_dt_bias, gd_norm_w, w_out_a, w_out_b, w_out):
    off_gate = RW_SHIFT_WIDTH
    off_conv = off_gate + RW_WIDTH
    off_beta = off_conv + GD_CONV_WIDTH
    off_ggate = off_beta + 2 * GD_HEADS
    off_merge = off_ggate + GD_WIDTH
    w_bf = w_in.astype(BF16)
    pad = jnp.zeros((D_MODEL, LANES - 2 * GD_HEADS), BF16)
    w2a2 = jnp.zeros((2 * RW_LORA, 2 * RW_WIDTH), F32)
    w2a2 = w2a2.at[:RW_LORA, :RW_WIDTH].set(rw_w2).at[RW_LORA:, RW_WIDTH:].set(rw_a2)
    lane_row = lambda vec: jnp.zeros((1, LANES), F32).at[0, GD_HEADS:2 * GD_HEADS].set(vec)
    row = lambda a: a.reshape(1, -1).astype(F32)
    return {
        'w_rw': w_bf[:, :off_conv],
        'w_gd': jnp.concatenate([w_bf[:, off_conv:off_ggate], pad, w_bf[:, off_ggate:off_merge]], axis=1),
        'w_mg': w_bf[:, off_merge:],
        'mu': row(rw_mu), 'w0': row(rw_w0), 'a0': row(rw_a0), 'k_k': row(rw_k_k), 'k_a': row(rw_k_a),
        'r_k': row(rw_r_k), 'ln_w': row(rw_ln_w), 'ln_b': row(rw_ln_b),
        'w2a2': w2a2.astype(BF16),
        'conv_w': gd_conv_w.astype(F32), 'a_log': lane_row(gd_a_log), 'dt_bias': lane_row(gd_dt_bias),
        'gn_w': row(jnp.tile(gd_norm_w, GD_HEADS)),
        'w_oa': w_out_a.astype(BF16), 'w_ob': w_out_b.astype(BF16), 'w_o': w_out.astype(BF16),
        'norm_post': row(norm_post),
    }


def _pair_block_diag(s):
    b = s.shape[0]
    s5 = s.reshape(b, N_PAIRS, 2, RW_HEAD_DIM, RW_HEAD_DIM)
    bd = jnp.einsum('bphvk,hg->bphvgk', s5, jnp.eye(2, dtype=s.dtype))
    return bd.reshape(b, N_PAIRS, LANES, LANES)


def _pair_diag_blocks(bd):
    b = bd.shape[0]
    bd6 = bd.reshape(b, N_PAIRS, 2, RW_HEAD_DIM, 2, RW_HEAD_DIM)
    s5 = jnp.einsum('bphvgk,hg->bphvk', bd6, jnp.eye(2, dtype=bd.dtype))
    return s5.reshape(b, RW_HEADS, RW_HEAD_DIM, RW_HEAD_DIM)


def _trunk(x, lead, streams, chunk, n_sub, shift_prev, conv_prev, s_rw, s_gd, norm_pre, consts):
    total = x.shape[0] * (x.shape[1] + (0 if lead is None else lead.shape[0]))
    t = total // streams
    n_chunks = t // chunk
    cc = dict(consts, **_chunk_consts(chunk, n_sub))
    xn = _prenorm(x, lead, norm_pre)
    proj_rw = _proj(xn, consts['w_rw'])
    proj_gd = _proj(xn, consts['w_gd'])
    proj_mg = _proj(xn, consts['w_mg'])
    hist_rw = jnp.zeros((streams, SUBLANES, RW_SHIFT_WIDTH), F32).at[:, SUBLANES - 1].set(shift_prev)
    hist_gd = jnp.zeros((streams, SUBLANES, GD_CONV_WIDTH), F32).at[:, SUBLANES - (GD_CONV - 1):].set(conv_prev)
    y_rw, s_rw_bd, y_gd, s_gd_new = _recurrences(proj_rw, hist_rw, _pair_block_diag(s_rw), proj_gd, hist_gd, s_gd,
                                                 cc, streams, n_chunks, chunk, n_sub)
    x_new = _merge_out(x, lead, y_rw, y_gd, proj_mg, consts)
    shift_new = proj_rw[t - 1::t, :RW_SHIFT_WIDTH]
    conv_new = jnp.stack([proj_gd[t - (GD_CONV - 1) + j::t, :GD_CONV_WIDTH] for j in range(GD_CONV - 1)], axis=1)
    return x_new, (shift_new, _pair_diag_blocks(s_rw_bd), conv_new, s_gd_new)


def kernel(x_prompt, x_sample, state_rwkv_shift, state_rwkv_wkv, state_gdn_conv, state_gdn_ssm, meta_tokens, norm_pre, w_in, rw_mu, rw_w0, rw_w2, rw_a0, rw_a2, rw_k_k, rw_k_a, rw_r_k, rw_ln_w, rw_ln_b, gd_conv_w, gd_a_log, gd_dt_bias, gd_norm_w, w_out_a, w_out_b, w_out, norm_post):
    assert w_in.shape[0] == 1, "single layer: the meta rows' outputs are not carried to a next layer"
    l = 0
    bp, seq, _ = x_prompt.shape
    bs, dec_seq, _ = x_sample.shape
    n_front = (-(N_META + seq)) % (PROMPT_CHUNK * PROMPT_SUB)
    lead = jnp.concatenate([jnp.zeros((n_front, D_MODEL), F32), meta_tokens.astype(F32)], axis=0)
    consts = _layer_consts(norm_post[l], w_in[l], rw_mu[l], rw_w0[l], rw_w2[l], rw_a0[l], rw_a2[l],
                           rw_k_k[l], rw_k_a[l], rw_r_k[l], rw_ln_w[l], rw_ln_b[l], gd_conv_w[l],
                           gd_a_log[l], gd_dt_bias[l], gd_norm_w[l], w_out_a[l], w_out_b[l], w_out[l])
    y_prompt, new_p = _trunk(x_prompt, lead, bp, PROMPT_CHUNK, PROMPT_SUB,
                             jnp.zeros((bp, RW_SHIFT_WIDTH), F32),
                             jnp.zeros((bp, GD_CONV - 1, GD_CONV_WIDTH), F32),
                             jnp.zeros((bp, RW_HEADS, RW_HEAD_DIM, RW_HEAD_DIM), F32),
                             jnp.zeros((bp, GD_HEADS, GD_HEAD_DIM, GD_HEAD_DIM), F32),
                             norm_pre[l], consts)
    ys, new_s = _trunk(x_sample.reshape(1, bs * dec_seq, D_MODEL), None, bs, dec_seq, 1,
                       state_rwkv_shift[l], state_gdn_conv[l], state_rwkv_wkv[l], state_gdn_ssm[l],
                       norm_pre[l], consts)
    y_sample = ys.reshape(bs, dec_seq, D_MODEL)
    return (y_prompt, y_sample) + tuple(t[None] for t in new_p) + tuple(t[None] for t in new_s)
```

```python
import functools

import numpy as np
import jax
import jax.numpy as jnp
from jax import lax
from jax.experimental import pallas as pl
from jax.experimental.pallas import tpu as pltpu

F32 = jnp.float32
BF16 = jnp.bfloat16

D_MODEL = 1024
N_META = 16
NORM_EPS = 1e-6
RW_HEADS = 16
RW_HEAD_DIM = 64
RW_WIDTH = 1024
RW_LORA = 64
RW_SHIFT_WIDTH = 3 * RW_WIDTH + 2 * RW_LORA
RW_GN_EPS = 64e-5
GD_HEADS = 8
GD_HEAD_DIM = 128
GD_WIDTH = 1024
GD_CONV = 4
GD_CONV_WIDTH = 3 * GD_WIDTH
LANES = 128
SUBLANES = 8
N_PAIRS = RW_HEADS // 2
SEC_RW = RW_SHIFT_WIDTH + RW_WIDTH
SEC_GD = GD_CONV_WIDTH + LANES + GD_WIDTH
SEC_MG = 2 * D_MODEL
PROMPT_CHUNK = 64
INV_BLOCK = 16
PROMPT_SUB = 4
VMEM_LIMIT = 56 * 1024 * 1024
ROW_TILE = 512


def _dot(a, b):
    return jnp.dot(a.astype(BF16), b.astype(BF16), preferred_element_type=F32)


def _dot_nt(a, b):
    return lax.dot_general(a.astype(BF16), b.astype(BF16), (((1,), (1,)), ((), ())),
                           preferred_element_type=F32)


def _dot_tn(a, b):
    return lax.dot_general(a.astype(BF16), b.astype(BF16), (((0,), (0,)), ((), ())),
                           preferred_element_type=F32)


def _split3(x):
    hi = x.astype(BF16)
    r1 = x - hi.astype(F32)
    mid = r1.astype(BF16)
    lo = (r1 - mid.astype(F32)).astype(BF16)
    return hi, mid, lo


def _sel_r(x, m01):
    hi, mid, lo = _split3(x)
    d = lambda p: jnp.dot(p, m01, preferred_element_type=F32)
    return d(hi) + d(mid) + d(lo)


def _sel_l(m01, x):
    hi, mid, lo = _split3(x)
    d = lambda p: jnp.dot(m01, p, preferred_element_type=F32)
    return d(hi) + d(mid) + d(lo)


def _sel_tn(x, m01):
    hi, mid, lo = _split3(x)
    d = lambda p: lax.dot_general(p, m01, (((0,), (0,)), ((), ())), preferred_element_type=F32)
    return d(hi) + d(mid) + d(lo)


def _sigmoid(x):
    return 1.0 / (1.0 + jnp.exp(-x))


def _silu(x):
    return x * _sigmoid(x)


def _softplus(x):
    return jnp.maximum(x, 0.0) + jnp.log(1.0 + jnp.exp(-jnp.abs(x)))


def _group_sum(x, group):
    parts = []
    for c in range(x.shape[1] // LANES):
        xs = x[:, c * LANES:(c + 1) * LANES]
        if group == LANES:
            parts.append(jnp.broadcast_to(jnp.sum(xs, axis=-1, keepdims=True), xs.shape))
        else:
            assert 2 * group == LANES
            first = lax.broadcasted_iota(jnp.int32, xs.shape, 1) < group
            s0 = jnp.sum(jnp.where(first, xs, 0.0), axis=-1, keepdims=True)
            s1 = jnp.sum(jnp.where(first, 0.0, xs), axis=-1, keepdims=True)
            parts.append(jnp.where(first, s0, s1))
    return jnp.concatenate(parts, axis=1)


def _log2(n):
    assert n & (n - 1) == 0
    return n.bit_length() - 1


def _tri_masks(n, chunk):
    ii = lax.broadcasted_iota(jnp.int32, (n, n), 0)
    jj = lax.broadcasted_iota(jnp.int32, (n, n), 1)
    ti = ii & (chunk - 1)
    tj = jj & (chunk - 1)
    xor = ii ^ jj
    blk = min(INV_BLOCK, chunk)
    masks = {
        'eye': (ii == jj).astype(F32),
        'strict': ti > tj,
        'incl': ti >= tj,
        'blk': (xor >> _log2(blk)) == 0,
        'off': [],
    }
    s = blk
    while s < chunk:
        masks['off'].append((xor >> _log2(s)) == 1)
        s *= 2
    return masks


def _unit_lower_inverse(n_mats, masks):
    power = [jnp.where(masks['blk'], x, 0.0) for x in n_mats]
    inv = [masks['eye'] + x for x in power]
    width = 1
    blk_rows = min(INV_BLOCK, n_mats[0].shape[0])
    while 2 * width < blk_rows:
        power = [_dot(x, x) for x in power]
        inv = [i_ + _dot(i_, x) for i_, x in zip(inv, power)]
        width *= 2
    for off_mask in masks['off']:
        off = [jnp.where(off_mask, x, 0.0) for x in n_mats]
        tmp = [_dot(o_, i_) for o_, i_ in zip(off, inv)]
        inv = [i_ + _dot(i_, t_) for i_, t_ in zip(inv, tmp)]
    return inv


def _row_tiling(x, lead):
    seq = x.shape[1]
    tm = min(ROW_TILE, seq) if lead is None else lead.shape[0]
    assert seq % tm == 0
    n_lead = 0 if lead is None else 1
    return tm, seq // tm + n_lead, n_lead


def _prenorm_kernel(*refs, has_lead):
    x_ref, g_ref, o_ref = refs[0], refs[-2], refs[-1]
    x = x_ref[...]
    if has_lead:
        x = jnp.where(pl.program_id(1) == 0, refs[1][...], x)
    ms = jnp.mean(x * x, axis=-1, keepdims=True)
    o_ref[...] = (x * lax.rsqrt(ms + NORM_EPS) * g_ref[...]).astype(o_ref.dtype)


def _prenorm(x, lead, gain):
    batch = x.shape[0]
    tm, nb, n_lead = _row_tiling(x, lead)
    x_spec = pl.BlockSpec((None, tm, D_MODEL), lambda b, i: (b, jnp.maximum(i - n_lead, 0), 0))
    lead_specs = [pl.BlockSpec((tm, D_MODEL), lambda b, i: (0, 0))] if n_lead else []
    return pl.pallas_call(
        functools.partial(_prenorm_kernel, has_lead=bool(n_lead)),
        grid=(batch, nb),
        in_specs=[x_spec] + lead_specs + [pl.BlockSpec((1, D_MODEL), lambda b, i: (0, 0))],
        out_specs=pl.BlockSpec((tm, D_MODEL), lambda b, i: (b * nb + i, 0)),
        out_shape=jax.ShapeDtypeStruct((batch * nb * tm, D_MODEL), BF16),
        compiler_params=pltpu.CompilerParams(dimension_semantics=("arbitrary", "arbitrary"),
                                             vmem_limit_bytes=VMEM_LIMIT),
    )(x, *([lead] if n_lead else []), gain.reshape(1, D_MODEL))


def _proj_kernel(x_ref, w_ref, o_ref):
    o_ref[...] = jnp.dot(x_ref[...], w_ref[...], preferred_element_type=F32).astype(o_ref.dtype)


def _proj(xn, w):
    n = xn.shape[0]
    width = w.shape[1]
    tm = min(ROW_TILE, n)
    return pl.pallas_call(
        _proj_kernel,
        grid=(pl.cdiv(n, tm),),
        in_specs=[pl.BlockSpec((tm, D_MODEL), lambda i: (i, 0)),
                  pl.BlockSpec((D_MODEL, width), lambda i: (0, 0))],
        out_specs=pl.BlockSpec((tm, width), lambda i: (i, 0)),
        out_shape=jax.ShapeDtypeStruct((n, width), F32),
        compiler_params=pltpu.CompilerParams(dimension_semantics=("arbitrary",),
                                             vmem_limit_bytes=VMEM_LIMIT),
    )(xn, w)


N_RW_IN = 13
N_GD_IN = 10


def _recur_kernel(*refs, chunk, n_sub, n_steps):
    rw_in, gd_in = refs[:N_RW_IN], refs[N_RW_IN:N_RW_IN + N_GD_IN]
    y_rw, s_rw_out, hist_rw, y_gd, s_gd, hist_gd, s_bd = refs[N_RW_IN + N_GD_IN:]
    dh = RW_HEAD_DIM

    @pl.when(pl.program_id(1) == 0)
    def _():
        hist_rw[...] = rw_in[1][...]
        hist_gd[...] = gd_in[1][...]
        s_gd[...] = gd_in[2][...]
        zero = jnp.zeros((dh, dh), F32)
        for p in range(N_PAIRS):
            top = jnp.concatenate([rw_in[2][2 * p], zero], axis=1)
            bottom = jnp.concatenate([zero, rw_in[2][2 * p + 1]], axis=1)
            s_bd[p] = jnp.concatenate([top, bottom], axis=0)

    phases = [_rwkv_phases(rw_in[0], *rw_in[3:], y_rw, s_bd, hist_rw, chunk=chunk, n_sub=n_sub),
              _gdn_phases(gd_in[0], *gd_in[3:], y_gd, s_gd, hist_gd, chunk=chunk, n_sub=n_sub)]
    while phases:
        for ph in list(phases):
            if next(ph, 'done') == 'done':
                phases.remove(ph)

    @pl.when(pl.program_id(1) == n_steps - 1)
    def _():
        for p in range(N_PAIRS):
            blk = s_bd[p]
            s_rw_out[2 * p] = blk[:dh, :dh]
            s_rw_out[2 * p + 1] = blk[dh:, dh:]


def _rwkv_phases(x_ref, mu_ref, w0_ref, a0_ref, kk_ref, ka_ref, rk_ref, lnw_ref, lnb_ref, w2a2_ref, tril_ref,
                 y_ref, s_ref, hist, *, chunk, n_sub):
    L = chunk
    rows = L * n_sub
    x = x_ref[...]
    ps = x[:, :RW_SHIFT_WIDTH]
    gate = x[:, RW_SHIFT_WIDTH:]
    full = jnp.concatenate([hist[...], ps], axis=0)
    prev = pltpu.roll(full, 1, axis=0)[SUBLANES:]
    hist[...] = ps[rows - SUBLANES:]
    xs = ps + mu_ref[...] * (prev - ps)
    r = xs[:, :RW_WIDTH]
    k = xs[:, RW_WIDTH:2 * RW_WIDTH]
    v = xs[:, 2 * RW_WIDTH:3 * RW_WIDTH]
    z = xs[:, 3 * RW_WIDTH:]
    lane = lax.broadcasted_iota(jnp.int32, (rows, LANES), 1)
    lhs = jnp.where(lane < RW_LORA, jnp.tanh(z), z)
    wa = _dot(lhs, w2a2_ref[...])
    w_log = -_softplus(-(w0_ref[...] + wa[:, :RW_WIDTH])) - 0.5
    logd = -jnp.exp(w_log)
    a = _sigmoid(a0_ref[...] + wa[:, RW_WIDTH:])
    kkr = k * kk_ref[...]
    kk = kkr * lax.rsqrt(_group_sum(kkr * kkr, RW_HEAD_DIM) + 1e-6)
    k2 = k * (1.0 + (a - 1.0) * ka_ref[...])
    bv = kk * a
    lc = _sel_l(tril_ref[...], logd)
    e_pos = jnp.exp(lc)
    e_neg = jnp.exp(-lc)
    e_last = [jnp.exp(lc[(s + 1) * L - 1:(s + 1) * L]) for s in range(n_sub)]
    e_last_rows = jnp.concatenate([jnp.broadcast_to(e, (L, RW_WIDTH)) for e in e_last], axis=0)
    at = -kk * jnp.exp(lc - logd)
    rt = r * e_pos
    bt = bv * e_neg
    kt = k2 * e_neg
    bh = bt * e_last_rows
    kh = kt * e_last_rows
    yield

    n = 2 * L
    masks = _tri_masks(n, L)
    first = lax.broadcasted_iota(jnp.int32, (L, LANES), 1) < RW_HEAD_DIM

    def stack(xp):
        return jnp.concatenate([jnp.where(first, xp, 0.0).astype(BF16),
                                jnp.where(first, 0.0, xp).astype(BF16)], axis=0)

    items = [(slice(s * L, (s + 1) * L), slice(p * LANES, (p + 1) * LANES))
             for s in range(n_sub) for p in range(N_PAIRS)]
    v_st = [stack(v[rs, sl]) for rs, sl in items]
    ar = [jnp.concatenate([stack(at[rs, sl]), stack(rt[rs, sl])], axis=0) for rs, sl in items]
    bk = [jnp.concatenate([stack(bt[rs, sl]), stack(kt[rs, sl])], axis=0) for rs, sl in items]
    bkh = [jnp.concatenate([stack(bh[rs, sl]), stack(kh[rs, sl])], axis=0) for rs, sl in items]
    m = [_dot_nt(x_, y_) for x_, y_ in zip(ar, bk)]
    n_ab = [jnp.where(masks['strict'], x_[:n, :n], 0.0) for x_ in m]
    t_inv = _unit_lower_inverse(n_ab, masks)
    akv = [_dot(jnp.where(masks['strict'], x_[:n, n:], 0.0), v_) for x_, v_ in zip(m, v_st)]
    a_r = [jnp.concatenate([jnp.where(masks['incl'], x_[n:, :n], 0.0).astype(BF16),
                            jnp.where(masks['incl'], x_[n:, n:], 0.0).astype(BF16)], axis=1) for x_ in m]
    yield
    state = [s_ref[p] for p in range(N_PAIRS)]
    y_rows = []
    for s in range(n_sub):
        it = range(s * N_PAIRS, (s + 1) * N_PAIRS)
        ars = [_dot_nt(ar[i], s_) for i, s_ in zip(it, state)]
        u = [_dot(t_inv[i], x_[:n] + akv[i]) for i, x_ in zip(it, ars)]
        uv = [jnp.concatenate([u_.astype(BF16), v_st[i]], axis=0) for i, u_ in zip(it, u)]
        y_st = [x_[n:] + _dot(a_r[i], uv_) for i, x_, uv_ in zip(it, ars, uv)]
        y_rows.append(jnp.concatenate([x_[:L] + x_[L:] for x_ in y_st], axis=1))
        state = [s_ * e_last[s][:, items[i][1]] + _dot_tn(uv_, bkh[i]) for i, s_, uv_ in zip(it, state, uv)]
        yield
    for p in range(N_PAIRS):
        s_ref[p] = state[p]

    y = jnp.concatenate(y_rows, axis=0)
    inv_dh = 1.0 / RW_HEAD_DIM
    mean = _group_sum(y, RW_HEAD_DIM) * inv_dh
    yc = y - mean
    var = _group_sum(yc * yc, RW_HEAD_DIM) * inv_dh
    yn = yc * lax.rsqrt(var + RW_GN_EPS) * lnw_ref[...] + lnb_ref[...]
    bonus = _group_sum(r * k2 * rk_ref[...], RW_HEAD_DIM) * v
    y_ref[...] = ((yn + bonus) * _silu(gate)).astype(y_ref.dtype)


def _gdn_phases(x_ref, cw_ref, alog_ref, dtb_ref, gnw_ref, eal_ref, tril_ref, triu_ref,
                y_ref, s_ref, hist, *, chunk, n_sub):
    L = chunk
    rows = L * n_sub
    x = x_ref[...]
    cin = x[:, :GD_CONV_WIDTH]
    ba = x[:, GD_CONV_WIDTH:GD_CONV_WIDTH + LANES]
    gate = x[:, GD_CONV_WIDTH + LANES:]
    full = jnp.concatenate([hist[...], cin], axis=0)
    hist[...] = cin[rows - SUBLANES:]
    cw = cw_ref[...]
    conv = cw[GD_CONV - 1:GD_CONV] * cin
    for d in range(1, GD_CONV):
        conv = conv + cw[GD_CONV - 1 - d:GD_CONV - d] * pltpu.roll(full, d, axis=0)[SUBLANES:]
    act = _silu(conv)
    q = act[:, :GD_WIDTH]
    k = act[:, GD_WIDTH:2 * GD_WIDTH]
    v = act[:, 2 * GD_WIDTH:]
    q = q * lax.rsqrt(_group_sum(q * q, GD_HEAD_DIM) + 1e-6) * (GD_HEAD_DIM ** -0.5)
    k = k * lax.rsqrt(_group_sum(k * k, GD_HEAD_DIM) + 1e-6)
    beta = _sigmoid(ba)
    g = -jnp.exp(alog_ref[...]) * _softplus(ba + dtb_ref[...])
    lane_bcast = lambda t, j: jnp.broadcast_to(t[:, j:j + 1], (rows, LANES))
    beta_x = jnp.concatenate([lane_bcast(beta, h) for h in range(GD_HEADS)], axis=1)
    g_x = jnp.concatenate([lane_bcast(g, GD_HEADS + h) for h in range(GD_HEADS)], axis=1)
    gcol = _sel_l(tril_ref[...], g_x)
    g_l = _sel_r(g, eal_ref[...])
    grow = [_sel_tn(g_l[s * L:(s + 1) * L], triu_ref[...]) for s in range(n_sub)]
    masks = _tri_masks(L, L)
    eg_all = jnp.exp(gcol)
    yield

    items = [(s, h, slice(s * L, (s + 1) * L), slice(h * LANES, (h + 1) * LANES))
             for s in range(n_sub) for h in range(GD_HEADS)]
    kq = [jnp.concatenate([k[rs, sl].astype(BF16), q[rs, sl].astype(BF16)], axis=0) for _, _, rs, sl in items]
    kkqk = [_dot_nt(x_, k[rs, sl]) for x_, (_, _, rs, sl) in zip(kq, items)]
    dec = [jnp.where(masks['incl'], jnp.exp(gcol[rs, sl][:, :L] - grow[s][h * L:(h + 1) * L]), 0.0)
           for s, h, rs, sl in items]
    m = [jnp.where(masks['strict'], -(beta_x[rs, sl][:, :L] * d_ * x_[:L]), 0.0)
         for d_, x_, (_, _, rs, sl) in zip(dec, kkqk, items)]
    t_inv = _unit_lower_inverse(m, masks)
    qkd = [(x_[L:] * d_).astype(BF16) for x_, d_ in zip(kkqk, dec)]
    yield
    state = [s_ref[h] for h in range(GD_HEADS)]
    o_rows = []
    for s in range(n_sub):
        it = range(s * GD_HEADS, (s + 1) * GD_HEADS)
        kqs = [_dot(kq[i], s_) for i, s_ in zip(it, state)]
        delta = []
        for i, x_ in zip(it, kqs):
            _, _, rs, sl = items[i]
            delta.append(_dot(t_inv[i], beta_x[rs, sl] * (v[rs, sl] - eg_all[rs, sl] * x_[:L])))
        o_rows.append(jnp.concatenate(
            [eg_all[items[i][2], items[i][3]] * x_[L:] + _dot(qkd[i], dl) for i, x_, dl in zip(it, kqs, delta)],
            axis=1))
        new_state = []
        for i, s_, dl in zip(it, state, delta):
            _, _, rs, sl = items[i]
            g_c = gcol[rs, sl]
            g_last = g_c[L - 1:L]
            new_state.append(jnp.exp(g_last) * s_ + _dot_tn(k[rs, sl] * jnp.exp(g_last - g_c), dl))
        state = new_state
        yield
    for h in range(GD_HEADS):
        s_ref[h] = state[h]

    o = jnp.concatenate(o_rows, axis=0)
    ms = _group_sum(o * o, GD_HEAD_DIM) * (1.0 / GD_HEAD_DIM)
    on = o * lax.rsqrt(ms + NORM_EPS) * gnw_ref[...]
    y_ref[...] = (on * _silu(gate)).astype(y_ref.dtype)


def _recurrences(proj_rw, hist_rw, s_rw, proj_gd, hist_gd, s_gd, consts, batch, n_chunks, chunk, n_sub):
    L = chunk * n_sub
    n_steps = n_chunks // n_sub
    rows = batch * n_steps * L
    row = lambda w: pl.BlockSpec((1, w), lambda b, c: (0, 0))
    full = lambda a: pl.BlockSpec(a.shape, lambda b, c: (0,) * a.ndim)
    step_rows = lambda w: pl.BlockSpec((L, w), lambda b, c: (b * n_steps + c, 0))
    per_stream = lambda *shape: pl.BlockSpec((None,) + shape, lambda b, c: (b,) + (0,) * len(shape))
    rw_names = ['mu', 'w0', 'a0', 'k_k', 'k_a', 'r_k', 'ln_w', 'ln_b', 'w2a2', 'tril']
    gd_names = ['conv_w', 'a_log', 'dt_bias', 'gn_w', 'e_alpha_l', 'tril', 'triu']
    rw_state = per_stream(RW_HEADS, RW_HEAD_DIM, RW_HEAD_DIM)
    rw_specs = ([step_rows(SEC_RW), per_stream(SUBLANES, RW_SHIFT_WIDTH), rw_state,
                 row(RW_SHIFT_WIDTH)] + [row(RW_WIDTH)] * 7 + [full(consts['w2a2']), full(consts['tril'])])
    gd_specs = ([step_rows(SEC_GD), per_stream(SUBLANES, GD_CONV_WIDTH), per_stream(GD_HEADS, LANES, LANES)] +
                [full(consts[nm]) for nm in gd_names])
    assert len(rw_specs) == N_RW_IN and len(gd_specs) == N_GD_IN
    return pl.pallas_call(
        functools.partial(_recur_kernel, chunk=chunk, n_sub=n_sub, n_steps=n_steps),
        grid=(batch, n_steps),
        in_specs=rw_specs + gd_specs,
        out_specs=[step_rows(RW_WIDTH), rw_state, per_stream(SUBLANES, RW_SHIFT_WIDTH),
                   step_rows(GD_WIDTH), per_stream(GD_HEADS, LANES, LANES), per_stream(SUBLANES, GD_CONV_WIDTH)],
        out_shape=[jax.ShapeDtypeStruct((rows, RW_WIDTH), BF16),
                   jax.ShapeDtypeStruct((batch, RW_HEADS, RW_HEAD_DIM, RW_HEAD_DIM), F32),
                   jax.ShapeDtypeStruct((batch, SUBLANES, RW_SHIFT_WIDTH), F32),
                   jax.ShapeDtypeStruct((rows, GD_WIDTH), BF16),
                   jax.ShapeDtypeStruct((batch, GD_HEADS, LANES, LANES), F32),
                   jax.ShapeDtypeStruct((batch, SUBLANES, GD_CONV_WIDTH), F32)],
        scratch_shapes=[pltpu.VMEM((N_PAIRS, LANES, LANES), F32)],
        compiler_params=pltpu.CompilerParams(dimension_semantics=("arbitrary", "arbitrary"),
                                             vmem_limit_bytes=VMEM_LIMIT),
    )(proj_rw, hist_rw, s_rw, *[consts[nm] for nm in rw_names],
      proj_gd, hist_gd, s_gd, *[consts[nm] for nm in gd_names])


def _out_kernel(h_ref, yrw_ref, ygd_ref, mg_ref, woa_ref, wob_ref, wo_ref, g_ref, o_ref, *, n_lead):
    @pl.when(pl.program_id(1) >= n_lead)
    def _():
        mg = mg_ref[...]
        ya = jnp.dot(yrw_ref[...], woa_ref[...], preferred_element_type=F32)
        yb = jnp.dot(ygd_ref[...], wob_ref[...], preferred_element_type=F32)
        merged = _sigmoid(mg[:, :D_MODEL]) * ya + _sigmoid(mg[:, D_MODEL:]) * yb
        out = _dot(merged, wo_ref[...])
        ms = jnp.mean(out * out, axis=-1, keepdims=True)
        o_ref[...] = h_ref[...] + out * lax.rsqrt(ms + NORM_EPS) * g_ref[...]


def _merge_out(x, lead, y_rw, y_gd, proj_mg, consts):
    batch, seq, _ = x.shape
    tm, nb, n_lead = _row_tiling(x, lead)
    x_spec = pl.BlockSpec((None, tm, D_MODEL), lambda b, i: (b, jnp.maximum(i - n_lead, 0), 0))
    rows = lambda w: pl.BlockSpec((tm, w), lambda b, i: (b * nb + i, 0))
    wspec = pl.BlockSpec((D_MODEL, D_MODEL), lambda b, i: (0, 0))
    return pl.pallas_call(
        functools.partial(_out_kernel, n_lead=n_lead),
        grid=(batch, nb),
        in_specs=[x_spec, rows(RW_WIDTH), rows(GD_WIDTH), rows(SEC_MG), wspec, wspec, wspec,
                  pl.BlockSpec((1, D_MODEL), lambda b, i: (0, 0))],
        out_specs=x_spec,
        out_shape=jax.ShapeDtypeStruct((batch, seq, D_MODEL), F32),
        compiler_params=pltpu.CompilerParams(dimension_semantics=("arbitrary", "arbitrary"),
                                             vmem_limit_bytes=VMEM_LIMIT),
    )(x, y_rw, y_gd, proj_mg, consts['w_oa'], consts['w_ob'], consts['w_o'], consts['norm_post'])


def _chunk_consts(chunk, n_sub):
    L = chunk
    tril = np.tril(np.ones((L, L), np.float32))
    ea_l = np.zeros((LANES, GD_HEADS * L), np.float32)
    for h in range(GD_HEADS):
        ea_l[GD_HEADS + h, h * L:(h + 1) * L] = 1.0
    return {'tril': jnp.asarray(np.kron(np.eye(n_sub, dtype=np.float32), tril), BF16),
            'triu': jnp.asarray(tril.T, BF16), 'e_alpha_l': jnp.asarray(ea_l, BF16)}


def _layer_consts(norm_post, w_in, rw_mu, rw_w0, rw_w2, rw_a0, rw_a2, rw_k_k, rw_k_a, rw_r_k, rw_ln_w,
                  rw_ln_b, gd_conv_w, gd_a_log, gd_dt_bias, gd_norm_w, w_out_a, w_out_b, w_out):
    off_gate = RW_SHIFT_WIDTH
    off_conv = off_gate + RW_WIDTH
    off_beta = off_conv + GD_CONV_WIDTH
    off_ggate = off_beta + 2 * GD_HEADS
    off_merge = off_ggate + GD_WIDTH
    w_bf = w_in.astype(BF16)
    pad = jnp.zeros((D_MODEL, LANES - 2 * GD_HEADS), BF16)
    w2a2 = jnp.zeros((2 * RW_LORA, 2 * RW_WIDTH), F32)
    w2a2 = w2a2.at[:RW_LORA, :RW_WIDTH].set(rw_w2).at[RW_LORA:, RW_WIDTH:].set(rw_a2)
    lane_row = lambda vec: jnp.zeros((1, LANES), F32).at[0, GD_HEADS:2 * GD_HEADS].set(vec)
    row = lambda a: a.reshape(1, -1).astype(F32)
    return {
        'w_rw': w_bf[:, :off_conv],
        'w_gd': jnp.concatenate([w_bf[:, off_conv:off_ggate], pad, w_bf[:, off_ggate:off_merge]], axis=1),
        'w_mg': w_bf[:, off_merge:],
        'mu': row(rw_mu), 'w0': row(rw_w0), 'a0': row(rw_a0), 'k_k': row(rw_k_k), 'k_a': row(rw_k_a),
        'r_k': row(rw_r_k), 'ln_w': row(rw_ln_w), 'ln_b': row(rw_ln_b),
        'w2a2': w2a2.astype(BF16),
        'conv_w': gd_conv_w.astype(F32), 'a_log': lane_row(gd_a_log), 'dt_bias': lane_row(gd_dt_bias),
        'gn_w': row(jnp.tile(gd_norm_w, GD_HEADS)),
        'w_oa': w_out_a.astype(BF16), 'w_ob': w_out_b.astype(BF16), 'w_o': w_out.astype(BF16),
        'norm_post': row(norm_post),
    }


def _trunk(x, lead, streams, chunk, n_sub, shift_prev, conv_prev, s_rw, s_gd, norm_pre, consts):
    total = x.shape[0] * (x.shape[1] + (0 if lead is None else lead.shape[0]))
    t = total // streams
    n_chunks = t // chunk
    cc = dict(consts, **_chunk_consts(chunk, n_sub))
    xn = _prenorm(x, lead, norm_pre)
    proj_rw = _proj(xn, consts['w_rw'])
    proj_gd = _proj(xn, consts['w_gd'])
    proj_mg = _proj(xn, consts['w_mg'])
    hist_rw = jnp.zeros((streams, SUBLANES, RW_SHIFT_WIDTH), F32).at[:, SUBLANES - 1].set(shift_prev)
    hist_gd = jnp.zeros((streams, SUBLANES, GD_CONV_WIDTH), F32).at[:, SUBLANES - (GD_CONV - 1):].set(conv_prev)
    y_rw, s_rw_new, last_rw, y_gd, s_gd_new, last_gd = _recurrences(
        proj_rw, hist_rw, s_rw, proj_gd, hist_gd, s_gd, cc, streams, n_chunks, chunk, n_sub)
    x_new = _merge_out(x, lead, y_rw, y_gd, proj_mg, consts)
    shift_new = last_rw[:, SUBLANES - 1]
    conv_new = last_gd[:, SUBLANES - (GD_CONV - 1):]
    return x_new, (shift_new, s_rw_new, conv_new, s_gd_new)


def kernel(x_prompt, x_sample, state_rwkv_shift, state_rwkv_wkv, state_gdn_conv, state_gdn_ssm, meta_tokens, norm_pre, w_in, rw_mu, rw_w0, rw_w2, rw_a0, rw_a2, rw_k_k, rw_k_a, rw_r_k, rw_ln_w, rw_ln_b, gd_conv_w, gd_a_log, gd_dt_bias, gd_norm_w, w_out_a, w_out_b, w_out, norm_post):
    assert w_in.shape[0] == 1, "single layer: the meta rows' outputs are not carried to a next layer"
    l = 0
    bp, seq, _ = x_prompt.shape
    bs, dec_seq, _ = x_sample.shape
    n_front = (-(N_META + seq)) % (PROMPT_CHUNK * PROMPT_SUB)
    lead = jnp.concatenate([jnp.zeros((n_front, D_MODEL), F32), meta_tokens.astype(F32)], axis=0)
    consts = _layer_consts(norm_post[l], w_in[l], rw_mu[l], rw_w0[l], rw_w2[l], rw_a0[l], rw_a2[l],
                           rw_k_k[l], rw_k_a[l], rw_r_k[l], rw_ln_w[l], rw_ln_b[l], gd_conv_w[l],
                           gd_a_log[l], gd_dt_bias[l], gd_norm_w[l], w_out_a[l], w_out_b[l], w_out[l])
    y_prompt, new_p = _trunk(x_prompt, lead, bp, PROMPT_CHUNK, PROMPT_SUB,
                             jnp.zeros((bp, RW_SHIFT_WIDTH), F32),
                             jnp.zeros((bp, GD_CONV - 1, GD_CONV_WIDTH), F32),
                             jnp.zeros((bp, RW_HEADS, RW_HEAD_DIM, RW_HEAD_DIM), F32),
                             jnp.zeros((bp, GD_HEADS, GD_HEAD_DIM, GD_HEAD_DIM), F32),
                             norm_pre[l], consts)
    ys, new_s = _trunk(x_sample.reshape(1, bs * dec_seq, D_MODEL), None, bs, dec_seq, 1,
                       state_rwkv_shift[l], state_gdn_conv[l], state_rwkv_wkv[l], state_gdn_ssm[l],
                       norm_pre[l], consts)
    y_sample = ys.reshape(bs, dec_seq, D_MODEL)
    return (y_prompt, y_sample) + tuple(t[None] for t in new_p) + tuple(t[None] for t in new_s)
```

```python
import functools

import numpy as np
import jax
import jax.numpy as jnp
from jax import lax
from jax.experimental import pallas as pl
from jax.experimental.pallas import tpu as pltpu

F32 = jnp.float32
BF16 = jnp.bfloat16

D_MODEL = 1024
N_META = 16
NORM_EPS = 1e-6
RW_HEADS = 16
RW_HEAD_DIM = 64
RW_WIDTH = 1024
RW_LORA = 64
RW_SHIFT_WIDTH = 3 * RW_WIDTH + 2 * RW_LORA
RW_GN_EPS = 64e-5
GD_HEADS = 8
GD_HEAD_DIM = 128
GD_WIDTH = 1024
GD_CONV = 4
GD_CONV_WIDTH = 3 * GD_WIDTH
LANES = 128
SUBLANES = 8
N_PAIRS = RW_HEADS // 2
SEC_RW = RW_SHIFT_WIDTH + RW_WIDTH
SEC_GD = GD_CONV_WIDTH + LANES + GD_WIDTH
SEC_MG = 2 * D_MODEL
PROMPT_CHUNK = 64
INV_BLOCK = 16
PROMPT_SUB = 4
VMEM_LIMIT = 56 * 1024 * 1024
ROW_TILE = 512


def _dot(a, b):
    return jnp.dot(a.astype(BF16), b.astype(BF16), preferred_element_type=F32)


def _dot_nt(a, b):
    return lax.dot_general(a.astype(BF16), b.astype(BF16), (((1,), (1,)), ((), ())),
                           preferred_element_type=F32)


def _dot_tn(a, b):
    return lax.dot_general(a.astype(BF16), b.astype(BF16), (((0,), (0,)), ((), ())),
                           preferred_element_type=F32)


def _split3(x):
    hi = x.astype(BF16)
    r1 = x - hi.astype(F32)
    mid = r1.astype(BF16)
    lo = (r1 - mid.astype(F32)).astype(BF16)
    return hi, mid, lo


def _sel_r(x, m01):
    hi, mid, lo = _split3(x)
    d = lambda p: jnp.dot(p, m01, preferred_element_type=F32)
    return d(hi) + d(mid) + d(lo)


def _sel_l(m01, x):
    hi, mid, lo = _split3(x)
    d = lambda p: jnp.dot(m01, p, preferred_element_type=F32)
    return d(hi) + d(mid) + d(lo)


def _sel_tn(x, m01):
    hi, mid, lo = _split3(x)
    d = lambda p: lax.dot_general(p, m01, (((0,), (0,)), ((), ())), preferred_element_type=F32)
    return d(hi) + d(mid) + d(lo)


def _sigmoid(x):
    return 1.0 / (1.0 + jnp.exp(-x))


def _silu(x):
    return x * _sigmoid(x)


def _softplus(x):
    return jnp.maximum(x, 0.0) + jnp.log(1.0 + jnp.exp(-jnp.abs(x)))


def _group_sum(x, group):
    parts = []
    for c in range(x.shape[1] // LANES):
        xs = x[:, c * LANES:(c + 1) * LANES]
        if group == LANES:
            parts.append(jnp.broadcast_to(jnp.sum(xs, axis=-1, keepdims=True), xs.shape))
        else:
            assert 2 * group == LANES
            first = lax.broadcasted_iota(jnp.int32, xs.shape, 1) < group
            s0 = jnp.sum(jnp.where(first, xs, 0.0), axis=-1, keepdims=True)
            s1 = jnp.sum(jnp.where(first, 0.0, xs), axis=-1, keepdims=True)
            parts.append(jnp.where(first, s0, s1))
    return jnp.concatenate(parts, axis=1)


def _log2(n):
    assert n & (n - 1) == 0
    return n.bit_length() - 1


def _tri_masks(n, chunk):
    ii = lax.broadcasted_iota(jnp.int32, (n, n), 0)
    jj = lax.broadcasted_iota(jnp.int32, (n, n), 1)
    ti = ii & (chunk - 1)
    tj = jj & (chunk - 1)
    xor = ii ^ jj
    blk = min(INV_BLOCK, chunk)
    masks = {
        'eye': (ii == jj).astype(F32),
        'strict': ti > tj,
        'incl': ti >= tj,
        'blk': (xor >> _log2(blk)) == 0,
        'off': [],
    }
    s = blk
    while s < chunk:
        masks['off'].append((xor >> _log2(s)) == 1)
        s *= 2
    return masks


def _unit_lower_inverse(n_mats, masks):
    power = [jnp.where(masks['blk'], x, 0.0) for x in n_mats]
    inv = [masks['eye'] + x for x in power]
    width = 1
    blk_rows = min(INV_BLOCK, n_mats[0].shape[0])
    while 2 * width < blk_rows:
        power = [_dot(x, x) for x in power]
        inv = [i_ + _dot(i_, x) for i_, x in zip(inv, power)]
        width *= 2
    for off_mask in masks['off']:
        off = [jnp.where(off_mask, x, 0.0) for x in n_mats]
        tmp = [_dot(o_, i_) for o_, i_ in zip(off, inv)]
        inv = [i_ + _dot(i_, t_) for i_, t_ in zip(inv, tmp)]
    return inv


def _row_tiling(x, lead):
    seq = x.shape[1]
    tm = min(ROW_TILE, seq) if lead is None else lead.shape[0]
    assert seq % tm == 0
    n_lead = 0 if lead is None else 1
    return tm, seq // tm + n_lead, n_lead


def _prenorm_kernel(*refs, has_lead):
    x_ref, g_ref, o_ref = refs[0], refs[-2], refs[-1]
    x = x_ref[...]
    if has_lead:
        x = jnp.where(pl.program_id(1) == 0, refs[1][...], x)
    ms = jnp.mean(x * x, axis=-1, keepdims=True)
    o_ref[...] = (x * lax.rsqrt(ms + NORM_EPS) * g_ref[...]).astype(o_ref.dtype)


def _prenorm(x, lead, gain):
    batch = x.shape[0]
    tm, nb, n_lead = _row_tiling(x, lead)
    x_spec = pl.BlockSpec((None, tm, D_MODEL), lambda b, i: (b, jnp.maximum(i - n_lead, 0), 0))
    lead_specs = [pl.BlockSpec((tm, D_MODEL), lambda b, i: (0, 0))] if n_lead else []
    return pl.pallas_call(
        functools.partial(_prenorm_kernel, has_lead=bool(n_lead)),
        grid=(batch, nb),
        in_specs=[x_spec] + lead_specs + [pl.BlockSpec((1, D_MODEL), lambda b, i: (0, 0))],
        out_specs=pl.BlockSpec((tm, D_MODEL), lambda b, i: (b * nb + i, 0)),
        out_shape=jax.ShapeDtypeStruct((batch * nb * tm, D_MODEL), BF16),
        compiler_params=pltpu.CompilerParams(dimension_semantics=("arbitrary", "arbitrary"),
                                             vmem_limit_bytes=VMEM_LIMIT),
    )(x, *([lead] if n_lead else []), gain.reshape(1, D_MODEL))


def _proj_kernel(x_ref, w_ref, o_ref):
    o_ref[...] = jnp.dot(x_ref[...], w_ref[...], preferred_element_type=F32).astype(o_ref.dtype)


def _proj(xn, w):
    n = xn.shape[0]
    width = w.shape[1]
    tm = min(ROW_TILE, n)
    return pl.pallas_call(
        _proj_kernel,
        grid=(pl.cdiv(n, tm),),
        in_specs=[pl.BlockSpec((tm, D_MODEL), lambda i: (i, 0)),
                  pl.BlockSpec((D_MODEL, width), lambda i: (0, 0))],
        out_specs=pl.BlockSpec((tm, width), lambda i: (i, 0)),
        out_shape=jax.ShapeDtypeStruct((n, width), F32),
        compiler_params=pltpu.CompilerParams(dimension_semantics=("arbitrary",),
                                             vmem_limit_bytes=VMEM_LIMIT),
    )(xn, w)


N_RW_IN = 13
N_GD_IN = 10


def _recur_kernel(*refs, chunk, n_sub, n_steps):
    rw_in, gd_in = refs[:N_RW_IN], refs[N_RW_IN:N_RW_IN + N_GD_IN]
    y_rw, s_rw_out, hist_rw, y_gd, s_gd, hist_gd, s_bd = refs[N_RW_IN + N_GD_IN:]
    dh = RW_HEAD_DIM

    @pl.when(pl.program_id(1) == 0)
    def _():
        hist_rw[...] = rw_in[1][...]
        hist_gd[...] = gd_in[1][...]
        s_gd[...] = gd_in[2][...]
        zero = jnp.zeros((dh, dh), F32)
        for p in range(N_PAIRS):
            top = jnp.concatenate([rw_in[2][2 * p], zero], axis=1)
            bottom = jnp.concatenate([zero, rw_in[2][2 * p + 1]], axis=1)
            s_bd[p] = jnp.concatenate([top, bottom], axis=0)

    phases = [_rwkv_phases(rw_in[0], *rw_in[3:], y_rw, s_bd, hist_rw, chunk=chunk, n_sub=n_sub),
              _gdn_phases(gd_in[0], *gd_in[3:], y_gd, s_gd, hist_gd, chunk=chunk, n_sub=n_sub)]
    while phases:
        for ph in list(phases):
            if next(ph, 'done') == 'done':
                phases.remove(ph)

    @pl.when(pl.program_id(1) == n_steps - 1)
    def _():
        for p in range(N_PAIRS):
            blk = s_bd[p]
            s_rw_out[2 * p] = blk[:dh, :dh]
            s_rw_out[2 * p + 1] = blk[dh:, dh:]


def _rwkv_phases(x_ref, mu_ref, w0_ref, a0_ref, kk_ref, ka_ref, rk_ref, lnw_ref, lnb_ref, w2a2_ref, tril_ref,
                 y_ref, s_ref, hist, *, chunk, n_sub):
    L = chunk
    rows = L * n_sub

    def shifted(c0):
        cs = slice(c0, c0 + LANES)
        ps = x_ref[:, cs]
        prev = pltpu.roll(jnp.concatenate([hist[:, cs], ps], axis=0), 1, axis=0)[SUBLANES:]
        return ps + mu_ref[:, cs] * (prev - ps)

    z = shifted(3 * RW_WIDTH)
    lane = lax.broadcasted_iota(jnp.int32, (rows, LANES), 1)
    lhs = jnp.where(lane < RW_LORA, jnp.tanh(z), z).astype(BF16)
    wa = _dot(lhs, w2a2_ref[...])
    logd_all = -jnp.exp(-_softplus(-(w0_ref[...] + wa[:, :RW_WIDTH])) - 0.5)
    lc_all = _sel_l(tril_ref[...], logd_all)
    at, rt, bt, kt, bh, kh, v, bonus, e_last = [], [], [], [], [], [], [], [], []
    for p in range(N_PAIRS):
        sl = slice(p * LANES, (p + 1) * LANES)
        r_p = shifted(p * LANES)
        k_p = shifted(RW_WIDTH + p * LANES)
        v_p = shifted(2 * RW_WIDTH + p * LANES)
        logd = logd_all[:, sl]
        lc = lc_all[:, sl]
        a = _sigmoid(a0_ref[:, sl] + wa[:, RW_WIDTH + p * LANES:RW_WIDTH + (p + 1) * LANES])
        kkr = k_p * kk_ref[:, sl]
        kk = kkr * lax.rsqrt(_group_sum(kkr * kkr, RW_HEAD_DIM) + 1e-6)
        k2 = k_p * (1.0 + (a - 1.0) * ka_ref[:, sl])
        e_neg = jnp.exp(-lc)
        el = [jnp.exp(lc[(s + 1) * L - 1:(s + 1) * L]) for s in range(n_sub)]
        el_rows = jnp.concatenate([jnp.broadcast_to(e, (L, LANES)) for e in el], axis=0)
        at.append(-kk * jnp.exp(lc - logd))
        rt.append(r_p * jnp.exp(lc))
        bt.append(kk * a * e_neg)
        kt.append(k2 * e_neg)
        bh.append(bt[-1] * el_rows)
        kh.append(kt[-1] * el_rows)
        v.append(v_p)
        bonus.append(_group_sum(r_p * k2 * rk_ref[:, sl], RW_HEAD_DIM) * v_p)
        e_last.append(el)
    hist[...] = x_ref[rows - SUBLANES:, :RW_SHIFT_WIDTH]
    yield

    n = 2 * L
    masks = _tri_masks(n, L)
    first = lax.broadcasted_iota(jnp.int32, (L, LANES), 1) < RW_HEAD_DIM

    def stack(xp):
        return jnp.concatenate([jnp.where(first, xp, 0.0).astype(BF16),
                                jnp.where(first, 0.0, xp).astype(BF16)], axis=0)

    items = [(slice(s * L, (s + 1) * L), p) for s in range(n_sub) for p in range(N_PAIRS)]
    v_st = [stack(v[p][rs]) for rs, p in items]
    ar = [jnp.concatenate([stack(at[p][rs]), stack(rt[p][rs])], axis=0) for rs, p in items]
    bk = [jnp.concatenate([stack(bt[p][rs]), stack(kt[p][rs])], axis=0) for rs, p in items]
    bkh = [jnp.concatenate([stack(bh[p][rs]), stack(kh[p][rs])], axis=0) for rs, p in items]
    m = [_dot_nt(x_, y_) for x_, y_ in zip(ar, bk)]
    n_ab = [jnp.where(masks['strict'], x_[:n, :n], 0.0) for x_ in m]
    t_inv = _unit_lower_inverse(n_ab, masks)
    akv = [_dot(jnp.where(masks['strict'], x_[:n, n:], 0.0), v_) for x_, v_ in zip(m, v_st)]
    a_r = [jnp.concatenate([jnp.where(masks['incl'], x_[n:, :n], 0.0).astype(BF16),
                            jnp.where(masks['incl'], x_[n:, n:], 0.0).astype(BF16)], axis=1) for x_ in m]
    yield
    state = [s_ref[p] for p in range(N_PAIRS)]
    y_rows = []
    for s in range(n_sub):
        it = range(s * N_PAIRS, (s + 1) * N_PAIRS)
        ars = [_dot_nt(ar[i], s_) for i, s_ in zip(it, state)]
        u = [_dot(t_inv[i], x_[:n] + akv[i]) for i, x_ in zip(it, ars)]
        uv = [jnp.concatenate([u_.astype(BF16), v_st[i]], axis=0) for i, u_ in zip(it, u)]
        y_st = [x_[n:] + _dot(a_r[i], uv_) for i, x_, uv_ in zip(it, ars, uv)]
        y_rows.append([x_[:L] + x_[L:] for x_ in y_st])
        state = [s_ * e_last[items[i][1]][s] + _dot_tn(uv_, bkh[i]) for i, s_, uv_ in zip(it, state, uv)]
        yield
    for p in range(N_PAIRS):
        s_ref[p] = state[p]

    inv_dh = 1.0 / RW_HEAD_DIM
    for p in range(N_PAIRS):
        sl = slice(p * LANES, (p + 1) * LANES)
        y = jnp.concatenate([y_rows[s][p] for s in range(n_sub)], axis=0)
        yc = y - _group_sum(y, RW_HEAD_DIM) * inv_dh
        var = _group_sum(yc * yc, RW_HEAD_DIM) * inv_dh
        yn = yc * lax.rsqrt(var + RW_GN_EPS) * lnw_ref[:, sl] + lnb_ref[:, sl]
        gate = x_ref[:, RW_SHIFT_WIDTH + p * LANES:RW_SHIFT_WIDTH + (p + 1) * LANES]
        y_ref[:, sl] = ((yn + bonus[p]) * _silu(gate)).astype(y_ref.dtype)


def _gdn_phases(x_ref, cw_ref, alog_ref, dtb_ref, gnw_ref, eal_ref, tril_ref, triu_ref,
                y_ref, s_ref, hist, *, chunk, n_sub):
    L = chunk
    rows = L * n_sub
    x = x_ref[...]
    cin = x[:, :GD_CONV_WIDTH]
    ba = x[:, GD_CONV_WIDTH:GD_CONV_WIDTH + LANES]
    gate = x[:, GD_CONV_WIDTH + LANES:]
    full = jnp.concatenate([hist[...], cin], axis=0)
    hist[...] = cin[rows - SUBLANES:]
    cw = cw_ref[...]
    conv = cw[GD_CONV - 1:GD_CONV] * cin
    for d in range(1, GD_CONV):
        conv = conv + cw[GD_CONV - 1 - d:GD_CONV - d] * pltpu.roll(full, d, axis=0)[SUBLANES:]
    act = _silu(conv)
    q = act[:, :GD_WIDTH]
    k = act[:, GD_WIDTH:2 * GD_WIDTH]
    v = act[:, 2 * GD_WIDTH:]
    q = q * lax.rsqrt(_group_sum(q * q, GD_HEAD_DIM) + 1e-6) * (GD_HEAD_DIM ** -0.5)
    k = k * lax.rsqrt(_group_sum(k * k, GD_HEAD_DIM) + 1e-6)
    beta = _sigmoid(ba)
    g = -jnp.exp(alog_ref[...]) * _softplus(ba + dtb_ref[...])
    lane_bcast = lambda t, j: jnp.broadcast_to(t[:, j:j + 1], (rows, LANES))
    beta_x = jnp.concatenate([lane_bcast(beta, h) for h in range(GD_HEADS)], axis=1)
    g_x = jnp.concatenate([lane_bcast(g, GD_HEADS + h) for h in range(GD_HEADS)], axis=1)
    gcol = _sel_l(tril_ref[...], g_x)
    g_l = _sel_r(g, eal_ref[...])
    grow = [_sel_tn(g_l[s * L:(s + 1) * L], triu_ref[...]) for s in range(n_sub)]
    masks = _tri_masks(L, L)
    eg_all = jnp.exp(gcol)
    yield

    items = [(s, h, slice(s * L, (s + 1) * L), slice(h * LANES, (h + 1) * LANES))
             for s in range(n_sub) for h in range(GD_HEADS)]
    kq = [jnp.concatenate([k[rs, sl].astype(BF16), q[rs, sl].astype(BF16)], axis=0) for _, _, rs, sl in items]
    kkqk = [_dot_nt(x_, k[rs, sl]) for x_, (_, _, rs, sl) in zip(kq, items)]
    dec = [jnp.where(masks['incl'], jnp.exp(gcol[rs, sl][:, :L] - grow[s][h * L:(h + 1) * L]), 0.0)
           for s, h, rs, sl in items]
    m = [jnp.where(masks['strict'], -(beta_x[rs, sl][:, :L] * d_ * x_[:L]), 0.0)
         for d_, x_, (_, _, rs, sl) in zip(dec, kkqk, items)]
    t_inv = _unit_lower_inverse(m, masks)
    qkd = [(x_[L:] * d_).astype(BF16) for x_, d_ in zip(kkqk, dec)]
    yield
    state = [s_ref[h] for h in range(GD_HEADS)]
    o_rows = []
    for s in range(n_sub):
        it = range(s * GD_HEADS, (s + 1) * GD_HEADS)
        kqs = [_dot(kq[i], s_) for i, s_ in zip(it, state)]
        delta = []
        for i, x_ in zip(it, kqs):
            _, _, rs, sl = items[i]
            delta.append(_dot(t_inv[i], beta_x[rs, sl] * (v[rs, sl] - eg_all[rs, sl] * x_[:L])))
        o_rows.append(jnp.concatenate(
            [eg_all[items[i][2], items[i][3]] * x_[L:] + _dot(qkd[i], dl) for i, x_, dl in zip(it, kqs, delta)],
            axis=1))
        new_state = []
        for i, s_, dl in zip(it, state, delta):
            _, _, rs, sl = items[i]
            g_c = gcol[rs, sl]
            g_last = g_c[L - 1:L]
            new_state.append(jnp.exp(g_last) * s_ + _dot_tn(k[rs, sl] * jnp.exp(g_last - g_c), dl))
        state = new_state
        yield
    for h in range(GD_HEADS):
        s_ref[h] = state[h]

    o = jnp.concatenate(o_rows, axis=0)
    ms = _group_sum(o * o, GD_HEAD_DIM) * (1.0 / GD_HEAD_DIM)
    on = o * lax.rsqrt(ms + NORM_EPS) * gnw_ref[...]
    y_ref[...] = (on * _silu(gate)).astype(y_ref.dtype)


def _recurrences(proj_rw, hist_rw, s_rw, proj_gd, hist_gd, s_gd, consts, batch, n_chunks, chunk, n_sub):
    L = chunk * n_sub
    n_steps = n_chunks // n_sub
    rows = batch * n_steps * L
    row = lambda w: pl.BlockSpec((1, w), lambda b, c: (0, 0))
    full = lambda a: pl.BlockSpec(a.shape, lambda b, c: (0,) * a.ndim)
    step_rows = lambda w: pl.BlockSpec((L, w), lambda b, c: (b * n_steps + c, 0))
    per_stream = lambda *shape: pl.BlockSpec((None,) + shape, lambda b, c: (b,) + (0,) * len(shape))
    rw_names = ['mu', 'w0', 'a0', 'k_k', 'k_a', 'r_k', 'ln_w', 'ln_b', 'w2a2', 'tril']
    gd_names = ['conv_w', 'a_log', 'dt_bias', 'gn_w', 'e_alpha_l', 'tril', 'triu']
    rw_state = per_stream(RW_HEADS, RW_HEAD_DIM, RW_HEAD_DIM)
    rw_specs = ([step_rows(SEC_RW), per_stream(SUBLANES, RW_SHIFT_WIDTH), rw_state,
                 row(RW_SHIFT_WIDTH)] + [row(RW_WIDTH)] * 7 + [full(consts['w2a2']), full(consts['tril'])])
    gd_specs = ([step_rows(SEC_GD), per_stream(SUBLANES, GD_CONV_WIDTH), per_stream(GD_HEADS, LANES, LANES)] +
                [full(consts[nm]) for nm in gd_names])
    assert len(rw_specs) == N_RW_IN and len(gd_specs) == N_GD_IN
    return pl.pallas_call(
        functools.partial(_recur_kernel, chunk=chunk, n_sub=n_sub, n_steps=n_steps),
        grid=(batch, n_steps),
        in_specs=rw_specs + gd_specs,
        out_specs=[step_rows(RW_WIDTH), rw_state, per_stream(SUBLANES, RW_SHIFT_WIDTH),
                   step_rows(GD_WIDTH), per_stream(GD_HEADS, LANES, LANES), per_stream(SUBLANES, GD_CONV_WIDTH)],
        out_shape=[jax.ShapeDtypeStruct((rows, RW_WIDTH), BF16),
                   jax.ShapeDtypeStruct((batch, RW_HEADS, RW_HEAD_DIM, RW_HEAD_DIM), F32),
                   jax.ShapeDtypeStruct((batch, SUBLANES, RW_SHIFT_WIDTH), F32),
                   jax.ShapeDtypeStruct((rows, GD_WIDTH), BF16),
                   jax.ShapeDtypeStruct((batch, GD_HEADS, LANES, LANES), F32),
                   jax.ShapeDtypeStruct((batch, SUBLANES, GD_CONV_WIDTH), F32)],
        scratch_shapes=[pltpu.VMEM((N_PAIRS, LANES, LANES), F32)],
        compiler_params=pltpu.CompilerParams(dimension_semantics=("arbitrary", "arbitrary"),
                                             vmem_limit_bytes=VMEM_LIMIT),
    )(proj_rw, hist_rw, s_rw, *[consts[nm] for nm in rw_names],
      proj_gd, hist_gd, s_gd, *[consts[nm] for nm in gd_names])


def _out_kernel(h_ref, yrw_ref, ygd_ref, mg_ref, woa_ref, wob_ref, wo_ref, g_ref, o_ref, *, n_lead):
    @pl.when(pl.program_id(1) >= n_lead)
    def _():
        mg = mg_ref[...]
        ya = jnp.dot(yrw_ref[...], woa_ref[...], preferred_element_type=F32)
        yb = jnp.dot(ygd_ref[...], wob_ref[...], preferred_element_type=F32)
        merged = _sigmoid(mg[:, :D_MODEL]) * ya + _sigmoid(mg[:, D_MODEL:]) * yb
        out = _dot(merged, wo_ref[...])
        ms = jnp.mean(out * out, axis=-1, keepdims=True)
        o_ref[...] = h_ref[...] + out * lax.rsqrt(ms + NORM_EPS) * g_ref[...]


def _merge_out(x, lead, y_rw, y_gd, proj_mg, consts):
    batch, seq, _ = x.shape
    tm, nb, n_lead = _row_tiling(x, lead)
    x_spec = pl.BlockSpec((None, tm, D_MODEL), lambda b, i: (b, jnp.maximum(i - n_lead, 0), 0))
    rows = lambda w: pl.BlockSpec((tm, w), lambda b, i: (b * nb + i, 0))
    wspec = pl.BlockSpec((D_MODEL, D_MODEL), lambda b, i: (0, 0))
    return pl.pallas_call(
        functools.partial(_out_kernel, n_lead=n_lead),
        grid=(batch, nb),
        in_specs=[x_spec, rows(RW_WIDTH), rows(GD_WIDTH), rows(SEC_MG), wspec, wspec, wspec,
                  pl.BlockSpec((1, D_MODEL), lambda b, i: (0, 0))],
        out_specs=x_spec,
        out_shape=jax.ShapeDtypeStruct((batch, seq, D_MODEL), F32),
        compiler_params=pltpu.CompilerParams(dimension_semantics=("arbitrary", "arbitrary"),
                                             vmem_limit_bytes=VMEM_LIMIT),
    )(x, y_rw, y_gd, proj_mg, consts['w_oa'], consts['w_ob'], consts['w_o'], consts['norm_post'])


def _chunk_consts(chunk, n_sub):
    L = chunk
    tril = np.tril(np.ones((L, L), np.float32))
    ea_l = np.zeros((LANES, GD_HEADS * L), np.float32)
    for h in range(GD_HEADS):
        ea_l[GD_HEADS + h, h * L:(h + 1) * L] = 1.0
    return {'tril': jnp.asarray(np.kron(np.eye(n_sub, dtype=np.float32), tril), BF16),
            'triu': jnp.asarray(tril.T, BF16), 'e_alpha_l': jnp.asarray(ea_l, BF16)}


def _layer_consts(norm_post, w_in, rw_mu, rw_w0, rw_w2, rw_a0, rw_a2, rw_k_k, rw_k_a, rw_r_k, rw_ln_w,
                  rw_ln_b, gd_conv_w, gd_a_log, gd_dt_bias, gd_norm_w, w_out_a, w_out_b, w_out):
    off_gate = RW_SHIFT_WIDTH
    off_conv = off_gate + RW_WIDTH
    off_beta = off_conv + GD_CONV_WIDTH
    off_ggate = off_beta + 2 * GD_HEADS
    off_merge = off_ggate + GD_WIDTH
    w_bf = w_in.astype(BF16)
    pad = jnp.zeros((D_MODEL, LANES - 2 * GD_HEADS), BF16)
    w2a2 = jnp.zeros((2 * RW_LORA, 2 * RW_WIDTH), F32)
    w2a2 = w2a2.at[:RW_LORA, :RW_WIDTH].set(rw_w2).at[RW_LORA:, RW_WIDTH:].set(rw_a2)
    lane_row = lambda vec: jnp.zeros((1, LANES), F32).at[0, GD_HEADS:2 * GD_HEADS].set(vec)
    row = lambda a: a.reshape(1, -1).astype(F32)
    return {
        'w_rw': w_bf[:, :off_conv],
        'w_gd': jnp.concatenate([w_bf[:, off_conv:off_ggate], pad, w_bf[:, off_ggate:off_merge]], axis=1),
        'w_mg': w_bf[:, off_merge:],
        'mu': row(rw_mu), 'w0': row(rw_w0), 'a0': row(rw_a0), 'k_k': row(rw_k_k), 'k_a': row(rw_k_a),
        'r_k': row(rw_r_k), 'ln_w': row(rw_ln_w), 'ln_b': row(rw_ln_b),
        'w2a2': w2a2.astype(BF16),
        'conv_w': gd_conv_w.astype(F32), 'a_log': lane_row(gd_a_log), 'dt_bias': lane_row(gd_dt_bias),
        'gn_w': row(jnp.tile(gd_norm_w, GD_HEADS)),
        'w_oa': w_out_a.astype(BF16), 'w_ob': w_out_b.astype(BF16), 'w_o': w_out.astype(BF16),
        'norm_post': row(norm_post),
    }


def _trunk(x, lead, streams, chunk, n_sub, shift_prev, conv_prev, s_rw, s_gd, norm_pre, consts):
    total = x.shape[0] * (x.shape[1] + (0 if lead is None else lead.shape[0]))
    t = total // streams
    n_chunks = t // chunk
    cc = dict(consts, **_chunk_consts(chunk, n_sub))
    xn = _prenorm(x, lead, norm_pre)
    proj_rw = _proj(xn, consts['w_rw'])
    proj_gd = _proj(xn, consts['w_gd'])
    proj_mg = _proj(xn, consts['w_mg'])
    hist_rw = jnp.zeros((streams, SUBLANES, RW_SHIFT_WIDTH), F32).at[:, SUBLANES - 1].set(shift_prev)
    hist_gd = jnp.zeros((streams, SUBLANES, GD_CONV_WIDTH), F32).at[:, SUBLANES - (GD_CONV - 1):].set(conv_prev)
    y_rw, s_rw_new, last_rw, y_gd, s_gd_new, last_gd = _recurrences(
        proj_rw, hist_rw, s_rw, proj_gd, hist_gd, s_gd, cc, streams, n_chunks, chunk, n_sub)
    x_new = _merge_out(x, lead, y_rw, y_gd, proj_mg, consts)
    shift_new = last_rw[:, SUBLANES - 1]
    conv_new = last_gd[:, SUBLANES - (GD_CONV - 1):]
    return x_new, (shift_new, s_rw_new, conv_new, s_gd_new)


def kernel(x_prompt, x_sample, state_rwkv_shift, state_rwkv_wkv, state_gdn_conv, state_gdn_ssm, meta_tokens, norm_pre, w_in, rw_mu, rw_w0, rw_w2, rw_a0, rw_a2, rw_k_k, rw_k_a, rw_r_k, rw_ln_w, rw_ln_b, gd_conv_w, gd_a_log, gd_dt_bias, gd_norm_w, w_out_a, w_out_b, w_out, norm_post):
    assert w_in.shape[0] == 1, "single layer: the meta rows' outputs are not carried to a next layer"
    l = 0
    bp, seq, _ = x_prompt.shape
    bs, dec_seq, _ = x_sample.shape
    n_front = (-(N_META + seq)) % (PROMPT_CHUNK * PROMPT_SUB)
    lead = jnp.concatenate([jnp.zeros((n_front, D_MODEL), F32), meta_tokens.astype(F32)], axis=0)
    consts = _layer_consts(norm_post[l], w_in[l], rw_mu[l], rw_w0[l], rw_w2[l], rw_a0[l], rw_a2[l],
                           rw_k_k[l], rw_k_a[l], rw_r_k[l], rw_ln_w[l], rw_ln_b[l], gd_conv_w[l],
                           gd_a_log[l], gd_dt_bias[l], gd_norm_w[l], w_out_a[l], w_out_b[l], w_out[l])
    y_prompt, new_p = _trunk(x_prompt, lead, bp, PROMPT_CHUNK, PROMPT_SUB,
                             jnp.zeros((bp, RW_SHIFT_WIDTH), F32),
                             jnp.zeros((bp, GD_CONV - 1, GD_CONV_WIDTH), F32),
                             jnp.zeros((bp, RW_HEADS, RW_HEAD_DIM, RW_HEAD_DIM), F32),
                             jnp.zeros((bp, GD_HEADS, GD_HEAD_DIM, GD_HEAD_DIM), F32),
                             norm_pre[l], consts)
    ys, new_s = _trunk(x_sample.reshape(1, bs * dec_seq, D_MODEL), None, bs, dec_seq, 1,
                       state_rwkv_shift[l], state_gdn_conv[l], state_rwkv_wkv[l], state_gdn_ssm[l],
                       norm_pre[l], consts)
    y_sample = ys.reshape(bs, dec_seq, D_MODEL)
    return (y_prompt, y_sample) + tuple(t[None] for t in new_p) + tuple(t[None] for t in new_s)
```

```python
import functools

import numpy as np
import jax
import jax.numpy as jnp
from jax import lax
from jax.experimental import pallas as pl
from jax.experimental.pallas import tpu as pltpu

F32 = jnp.float32
BF16 = jnp.bfloat16

D_MODEL = 1024
N_META = 16
NORM_EPS = 1e-6
RW_HEADS = 16
RW_HEAD_DIM = 64
RW_WIDTH = 1024
RW_LORA = 64
RW_SHIFT_WIDTH = 3 * RW_WIDTH + 2 * RW_LORA
RW_GN_EPS = 64e-5
GD_HEADS = 8
GD_HEAD_DIM = 128
GD_WIDTH = 1024
GD_CONV = 4
GD_CONV_WIDTH = 3 * GD_WIDTH
LANES = 128
SUBLANES = 8
N_PAIRS = RW_HEADS // 2
SEC_RW = RW_SHIFT_WIDTH + RW_WIDTH
SEC_GD = GD_CONV_WIDTH + LANES + GD_WIDTH
SEC_MG = 2 * D_MODEL
PROMPT_CHUNK = 64
INV_BLOCK = 16
PROMPT_SUB = 4
VMEM_LIMIT = 56 * 1024 * 1024
ROW_TILE = 512


def _dot(a, b):
    return jnp.dot(a.astype(BF16), b.astype(BF16), preferred_element_type=F32)


def _dot_nt(a, b):
    return lax.dot_general(a.astype(BF16), b.astype(BF16), (((1,), (1,)), ((), ())),
                           preferred_element_type=F32)


def _dot_tn(a, b):
    return lax.dot_general(a.astype(BF16), b.astype(BF16), (((0,), (0,)), ((), ())),
                           preferred_element_type=F32)


def _split3(x):
    hi = x.astype(BF16)
    r1 = x - hi.astype(F32)
    mid = r1.astype(BF16)
    lo = (r1 - mid.astype(F32)).astype(BF16)
    return hi, mid, lo


def _sel_r(x, m01):
    hi, mid, lo = _split3(x)
    d = lambda p: jnp.dot(p, m01, preferred_element_type=F32)
    return d(hi) + d(mid) + d(lo)


def _sel_l(m01, x):
    hi, mid, lo = _split3(x)
    d = lambda p: jnp.dot(m01, p, preferred_element_type=F32)
    return d(hi) + d(mid) + d(lo)


def _sel_tn(x, m01):
    hi, mid, lo = _split3(x)
    d = lambda p: lax.dot_general(p, m01, (((0,), (0,)), ((), ())), preferred_element_type=F32)
    return d(hi) + d(mid) + d(lo)


def _sigmoid(x):
    return 1.0 / (1.0 + jnp.exp(-x))


def _silu(x):
    return x * _sigmoid(x)


def _softplus(x):
    return jnp.maximum(x, 0.0) + jnp.log(1.0 + jnp.exp(-jnp.abs(x)))


def _group_sum(x, group):
    parts = []
    for c in range(x.shape[1] // LANES):
        xs = x[:, c * LANES:(c + 1) * LANES]
        if group == LANES:
            parts.append(jnp.broadcast_to(jnp.sum(xs, axis=-1, keepdims=True), xs.shape))
        else:
            assert 2 * group == LANES
            first = lax.broadcasted_iota(jnp.int32, xs.shape, 1) < group
            s0 = jnp.sum(jnp.where(first, xs, 0.0), axis=-1, keepdims=True)
            s1 = jnp.sum(jnp.where(first, 0.0, xs), axis=-1, keepdims=True)
            parts.append(jnp.where(first, s0, s1))
    return jnp.concatenate(parts, axis=1)


def _log2(n):
    assert n & (n - 1) == 0
    return n.bit_length() - 1


def _tri_masks(n, chunk):
    ii = lax.broadcasted_iota(jnp.int32, (n, n), 0)
    jj = lax.broadcasted_iota(jnp.int32, (n, n), 1)
    ti = ii & (chunk - 1)
    tj = jj & (chunk - 1)
    xor = ii ^ jj
    blk = min(INV_BLOCK, chunk)
    masks = {
        'eye': (ii == jj).astype(F32),
        'strict': ti > tj,
        'incl': ti >= tj,
        'blk': (xor >> _log2(blk)) == 0,
        'off': [],
    }
    s = blk
    while s < chunk:
        masks['off'].append((xor >> _log2(s)) == 1)
        s *= 2
    return masks


def _unit_lower_inverse(n_mats, masks):
    power = [jnp.where(masks['blk'], x, 0.0) for x in n_mats]
    inv = [masks['eye'] + x for x in power]
    width = 1
    blk_rows = min(INV_BLOCK, n_mats[0].shape[0])
    while 2 * width < blk_rows:
        power = [_dot(x, x) for x in power]
        inv = [i_ + _dot(i_, x) for i_, x in zip(inv, power)]
        width *= 2
    n = n_mats[0].shape[0]
    size = blk_rows
    for off_mask in masks['off']:
        odd = [slice(r, r + size) for r in range(size, n, 2 * size)]
        take = lambda x: jnp.concatenate([x[sl] for sl in odd], axis=0)
        zero = jnp.zeros((size, n), F32)
        spread = lambda x: jnp.concatenate(
            [part for j in range(len(odd)) for part in (zero, x[j * size:(j + 1) * size])], axis=0)
        tmp = [_dot(take(jnp.where(off_mask, x, 0.0)), i_) for x, i_ in zip(n_mats, inv)]
        inv = [i_ + spread(_dot(take(i_), spread(t_))) for i_, t_ in zip(inv, tmp)]
        size *= 2
    return inv


def _row_tiling(x, lead):
    seq = x.shape[1]
    tm = min(ROW_TILE, seq) if lead is None else lead.shape[0]
    assert seq % tm == 0
    n_lead = 0 if lead is None else 1
    return tm, seq // tm + n_lead, n_lead


def _prenorm_kernel(*refs, has_lead):
    x_ref, g_ref, o_ref = refs[0], refs[-2], refs[-1]
    x = x_ref[...]
    if has_lead:
        x = jnp.where(pl.program_id(1) == 0, refs[1][...], x)
    ms = jnp.mean(x * x, axis=-1, keepdims=True)
    o_ref[...] = (x * lax.rsqrt(ms + NORM_EPS) * g_ref[...]).astype(o_ref.dtype)


def _prenorm(x, lead, gain):
    batch = x.shape[0]
    tm, nb, n_lead = _row_tiling(x, lead)
    x_spec = pl.BlockSpec((None, tm, D_MODEL), lambda b, i: (b, jnp.maximum(i - n_lead, 0), 0))
    lead_specs = [pl.BlockSpec((tm, D_MODEL), lambda b, i: (0, 0))] if n_lead else []
    return pl.pallas_call(
        functools.partial(_prenorm_kernel, has_lead=bool(n_lead)),
        grid=(batch, nb),
        in_specs=[x_spec] + lead_specs + [pl.BlockSpec((1, D_MODEL), lambda b, i: (0, 0))],
        out_specs=pl.BlockSpec((tm, D_MODEL), lambda b, i: (b * nb + i, 0)),
        out_shape=jax.ShapeDtypeStruct((batch * nb * tm, D_MODEL), BF16),
        compiler_params=pltpu.CompilerParams(dimension_semantics=("arbitrary", "arbitrary"),
                                             vmem_limit_bytes=VMEM_LIMIT),
    )(x, *([lead] if n_lead else []), gain.reshape(1, D_MODEL))


def _proj_kernel(x_ref, w_ref, o_ref):
    o_ref[...] = jnp.dot(x_ref[...], w_ref[...], preferred_element_type=F32).astype(o_ref.dtype)


def _proj(xn, w):
    n = xn.shape[0]
    width = w.shape[1]
    tm = min(ROW_TILE, n)
    return pl.pallas_call(
        _proj_kernel,
        grid=(pl.cdiv(n, tm),),
        in_specs=[pl.BlockSpec((tm, D_MODEL), lambda i: (i, 0)),
                  pl.BlockSpec((D_MODEL, width), lambda i: (0, 0))],
        out_specs=pl.BlockSpec((tm, width), lambda i: (i, 0)),
        out_shape=jax.ShapeDtypeStruct((n, width), F32),
        compiler_params=pltpu.CompilerParams(dimension_semantics=("arbitrary",),
                                             vmem_limit_bytes=VMEM_LIMIT),
    )(xn, w)


N_RW_IN = 13
N_GD_IN = 10


def _recur_kernel(*refs, chunk, n_sub, n_steps):
    rw_in, gd_in = refs[:N_RW_IN], refs[N_RW_IN:N_RW_IN + N_GD_IN]
    y_rw, s_rw_out, hist_rw, y_gd, s_gd, hist_gd, s_bd = refs[N_RW_IN + N_GD_IN:]
    dh = RW_HEAD_DIM

    @pl.when(pl.program_id(1) == 0)
    def _():
        hist_rw[...] = rw_in[1][...]
        hist_gd[...] = gd_in[1][...]
        s_gd[...] = gd_in[2][...]
        zero = jnp.zeros((dh, dh), F32)
        for p in range(N_PAIRS):
            top = jnp.concatenate([rw_in[2][2 * p], zero], axis=1)
            bottom = jnp.concatenate([zero, rw_in[2][2 * p + 1]], axis=1)
            s_bd[p] = jnp.concatenate([top, bottom], axis=0)

    phases = [_rwkv_phases(rw_in[0], *rw_in[3:], y_rw, s_bd, hist_rw, chunk=chunk, n_sub=n_sub),
              _gdn_phases(gd_in[0], *gd_in[3:], y_gd, s_gd, hist_gd, chunk=chunk, n_sub=n_sub)]
    while phases:
        for ph in list(phases):
            if next(ph, 'done') == 'done':
                phases.remove(ph)

    @pl.when(pl.program_id(1) == n_steps - 1)
    def _():
        for p in range(N_PAIRS):
            blk = s_bd[p]
            s_rw_out[2 * p] = blk[:dh, :dh]
            s_rw_out[2 * p + 1] = blk[dh:, dh:]


def _rwkv_phases(x_ref, mu_ref, w0_ref, a0_ref, kk_ref, ka_ref, rk_ref, lnw_ref, lnb_ref, w2a2_ref, tril_ref,
                 y_ref, s_ref, hist, *, chunk, n_sub):
    L = chunk
    rows = L * n_sub
    x = x_ref[...]
    ps = x[:, :RW_SHIFT_WIDTH]
    gate = x[:, RW_SHIFT_WIDTH:]
    full = jnp.concatenate([hist[...], ps], axis=0)
    prev = pltpu.roll(full, 1, axis=0)[SUBLANES:]
    hist[...] = ps[rows - SUBLANES:]
    xs = ps + mu_ref[...] * (prev - ps)
    r = xs[:, :RW_WIDTH]
    k = xs[:, RW_WIDTH:2 * RW_WIDTH]
    v = xs[:, 2 * RW_WIDTH:3 * RW_WIDTH]
    z = xs[:, 3 * RW_WIDTH:]
    lane = lax.broadcasted_iota(jnp.int32, (rows, LANES), 1)
    lhs = jnp.where(lane < RW_LORA, jnp.tanh(z), z)
    wa = _dot(lhs, w2a2_ref[...])
    w_log = -_softplus(-(w0_ref[...] + wa[:, :RW_WIDTH])) - 0.5
    logd = -jnp.exp(w_log)
    a = _sigmoid(a0_ref[...] + wa[:, RW_WIDTH:])
    kkr = k * kk_ref[...]
    kk = kkr * lax.rsqrt(_group_sum(kkr * kkr, RW_HEAD_DIM) + 1e-6)
    k2 = k * (1.0 + (a - 1.0) * ka_ref[...])
    bv = kk * a
    lc = _sel_l(tril_ref[...], logd)
    e_pos = jnp.exp(lc)
    e_neg = jnp.exp(-lc)
    e_last = [jnp.exp(lc[(s + 1) * L - 1:(s + 1) * L]) for s in range(n_sub)]
    e_last_rows = jnp.concatenate([jnp.broadcast_to(e, (L, RW_WIDTH)) for e in e_last], axis=0)
    at = -kk * jnp.exp(lc - logd)
    rt = r * e_pos
    bt = bv * e_neg
    kt = k2 * e_neg
    bh = bt * e_last_rows
    kh = kt * e_last_rows
    yield

    n = 2 * L
    masks = _tri_masks(n, L)
    first = lax.broadcasted_iota(jnp.int32, (L, LANES), 1) < RW_HEAD_DIM

    def stack(xp):
        return jnp.concatenate([jnp.where(first, xp, 0.0).astype(BF16),
                                jnp.where(first, 0.0, xp).astype(BF16)], axis=0)

    items = [(slice(s * L, (s + 1) * L), slice(p * LANES, (p + 1) * LANES))
             for s in range(n_sub) for p in range(N_PAIRS)]
    v_st = [stack(v[rs, sl]) for rs, sl in items]
    ar = [jnp.concatenate([stack(at[rs, sl]), stack(rt[rs, sl])], axis=0) for rs, sl in items]
    bk = [jnp.concatenate([stack(bt[rs, sl]), stack(kt[rs, sl])], axis=0) for rs, sl in items]
    bkh = [jnp.concatenate([stack(bh[rs, sl]), stack(kh[rs, sl])], axis=0) for rs, sl in items]
    m = [_dot_nt(x_, y_) for x_, y_ in zip(ar, bk)]
    n_ab = [jnp.where(masks['strict'], x_[:n, :n], 0.0) for x_ in m]
    t_inv = _unit_lower_inverse(n_ab, masks)
    akv = [_dot(jnp.where(masks['strict'], x_[:n, n:], 0.0), v_) for x_, v_ in zip(m, v_st)]
    a_r = [jnp.concatenate([jnp.where(masks['incl'], x_[n:, :n], 0.0).astype(BF16),
                            jnp.where(masks['incl'], x_[n:, n:], 0.0).astype(BF16)], axis=1) for x_ in m]
    yield
    state = [s_ref[p] for p in range(N_PAIRS)]
    y_rows = []
    for s in range(n_sub):
        it = range(s * N_PAIRS, (s + 1) * N_PAIRS)
        ars = [_dot_nt(ar[i], s_) for i, s_ in zip(it, state)]
        u = [_dot(t_inv[i], x_[:n] + akv[i]) for i, x_ in zip(it, ars)]
        uv = [jnp.concatenate([u_.astype(BF16), v_st[i]], axis=0) for i, u_ in zip(it, u)]
        y_st = [x_[n:] + _dot(a_r[i], uv_) for i, x_, uv_ in zip(it, ars, uv)]
        y_rows.append(jnp.concatenate([x_[:L] + x_[L:] for x_ in y_st], axis=1))
        state = [s_ * e_last[s][:, items[i][1]] + _dot_tn(uv_, bkh[i]) for i, s_, uv_ in zip(it, state, uv)]
        yield
    for p in range(N_PAIRS):
        s_ref[p] = state[p]

    y = jnp.concatenate(y_rows, axis=0)
    inv_dh = 1.0 / RW_HEAD_DIM
    mean = _group_sum(y, RW_HEAD_DIM) * inv_dh
    yc = y - mean
    var = _group_sum(yc * yc, RW_HEAD_DIM) * inv_dh
    yn = yc * lax.rsqrt(var + RW_GN_EPS) * lnw_ref[...] + lnb_ref[...]
    bonus = _group_sum(r * k2 * rk_ref[...], RW_HEAD_DIM) * v
    y_ref[...] = ((yn + bonus) * _silu(gate)).astype(y_ref.dtype)


def _gdn_phases(x_ref, cw_ref, alog_ref, dtb_ref, gnw_ref, eal_ref, tril_ref, triu_ref,
                y_ref, s_ref, hist, *, chunk, n_sub):
    L = chunk
    rows = L * n_sub
    x = x_ref[...]
    cin = x[:, :GD_CONV_WIDTH]
    ba = x[:, GD_CONV_WIDTH:GD_CONV_WIDTH + LANES]
    gate = x[:, GD_CONV_WIDTH + LANES:]
    full = jnp.concatenate([hist[...], cin], axis=0)
    hist[...] = cin[rows - SUBLANES:]
    cw = cw_ref[...]
    conv = cw[GD_CONV - 1:GD_CONV] * cin
    for d in range(1, GD_CONV):
        conv = conv + cw[GD_CONV - 1 - d:GD_CONV - d] * pltpu.roll(full, d, axis=0)[SUBLANES:]
    act = _silu(conv)
    q = act[:, :GD_WIDTH]
    k = act[:, GD_WIDTH:2 * GD_WIDTH]
    v = act[:, 2 * GD_WIDTH:]
    q = q * lax.rsqrt(_group_sum(q * q, GD_HEAD_DIM) + 1e-6) * (GD_HEAD_DIM ** -0.5)
    k = k * lax.rsqrt(_group_sum(k * k, GD_HEAD_DIM) + 1e-6)
    beta = _sigmoid(ba)
    g = -jnp.exp(alog_ref[...]) * _softplus(ba + dtb_ref[...])
    lane_bcast = lambda t, j: jnp.broadcast_to(t[:, j:j + 1], (rows, LANES))
    beta_x = jnp.concatenate([lane_bcast(beta, h) for h in range(GD_HEADS)], axis=1)
    g_x = jnp.concatenate([lane_bcast(g, GD_HEADS + h) for h in range(GD_HEADS)], axis=1)
    gcol = _sel_l(tril_ref[...], g_x)
    g_l = _sel_r(g, eal_ref[...])
    grow = [_sel_tn(g_l[s * L:(s + 1) * L], triu_ref[...]) for s in range(n_sub)]
    masks = _tri_masks(L, L)
    eg_all = jnp.exp(gcol)
    yield

    items = [(s, h, slice(s * L, (s + 1) * L), slice(h * LANES, (h + 1) * LANES))
             for s in range(n_sub) for h in range(GD_HEADS)]
    kq = [jnp.concatenate([k[rs, sl].astype(BF16), q[rs, sl].astype(BF16)], axis=0) for _, _, rs, sl in items]
    kkqk = [_dot_nt(x_, k[rs, sl]) for x_, (_, _, rs, sl) in zip(kq, items)]
    dec = [jnp.where(masks['incl'], jnp.exp(gcol[rs, sl][:, :L] - grow[s][h * L:(h + 1) * L]), 0.0)
           for s, h, rs, sl in items]
    m = [jnp.where(masks['strict'], -(beta_x[rs, sl][:, :L] * d_ * x_[:L]), 0.0)
         for d_, x_, (_, _, rs, sl) in zip(dec, kkqk, items)]
    t_inv = _unit_lower_inverse(m, masks)
    qkd = [(x_[L:] * d_).astype(BF16) for x_, d_ in zip(kkqk, dec)]
    yield
    state = [s_ref[h] for h in range(GD_HEADS)]
    o_rows = []
    for s in range(n_sub):
        it = range(s * GD_HEADS, (s + 1) * GD_HEADS)
        kqs = [_dot(kq[i], s_) for i, s_ in zip(it, state)]
        delta = []
        for i, x_ in zip(it, kqs):
            _, _, rs, sl = items[i]
            delta.append(_dot(t_inv[i], beta_x[rs, sl] * (v[rs, sl] - eg_all[rs, sl] * x_[:L])))
        o_rows.append(jnp.concatenate(
            [eg_all[items[i][2], items[i][3]] * x_[L:] + _dot(qkd[i], dl) for i, x_, dl in zip(it, kqs, delta)],
            axis=1))
        new_state = []
        for i, s_, dl in zip(it, state, delta):
            _, _, rs, sl = items[i]
            g_c = gcol[rs, sl]
            g_last = g_c[L - 1:L]
            new_state.append(jnp.exp(g_last) * s_ + _dot_tn(k[rs, sl] * jnp.exp(g_last - g_c), dl))
        state = new_state
        yield
    for h in range(GD_HEADS):
        s_ref[h] = state[h]

    o = jnp.concatenate(o_rows, axis=0)
    ms = _group_sum(o * o, GD_HEAD_DIM) * (1.0 / GD_HEAD_DIM)
    on = o * lax.rsqrt(ms + NORM_EPS) * gnw_ref[...]
    y_ref[...] = (on * _silu(gate)).astype(y_ref.dtype)


def _recurrences(proj_rw, hist_rw, s_rw, proj_gd, hist_gd, s_gd, consts, batch, n_chunks, chunk, n_sub):
    L = chunk * n_sub
    n_steps = n_chunks // n_sub
    rows = batch * n_steps * L
    row = lambda w: pl.BlockSpec((1, w), lambda b, c: (0, 0))
    full = lambda a: pl.BlockSpec(a.shape, lambda b, c: (0,) * a.ndim)
    step_rows = lambda w: pl.BlockSpec((L, w), lambda b, c: (b * n_steps + c, 0))
    per_stream = lambda *shape: pl.BlockSpec((None,) + shape, lambda b, c: (b,) + (0,) * len(shape))
    rw_names = ['mu', 'w0', 'a0', 'k_k', 'k_a', 'r_k', 'ln_w', 'ln_b', 'w2a2', 'tril']
    gd_names = ['conv_w', 'a_log', 'dt_bias', 'gn_w', 'e_alpha_l', 'tril', 'triu']
    rw_state = per_stream(RW_HEADS, RW_HEAD_DIM, RW_HEAD_DIM)
    rw_specs = ([step_rows(SEC_RW), per_stream(SUBLANES, RW_SHIFT_WIDTH), rw_state,
                 row(RW_SHIFT_WIDTH)] + [row(RW_WIDTH)] * 7 + [full(consts['w2a2']), full(consts['tril'])])
    gd_specs = ([step_rows(SEC_GD), per_stream(SUBLANES, GD_CONV_WIDTH), per_stream(GD_HEADS, LANES, LANES)] +
                [full(consts[nm]) for nm in gd_names])
    assert len(rw_specs) == N_RW_IN and len(gd_specs) == N_GD_IN
    return pl.pallas_call(
        functools.partial(_recur_kernel, chunk=chunk, n_sub=n_sub, n_steps=n_steps),
        grid=(batch, n_steps),
        in_specs=rw_specs + gd_specs,
        out_specs=[step_rows(RW_WIDTH), rw_state, per_stream(SUBLANES, RW_SHIFT_WIDTH),
                   step_rows(GD_WIDTH), per_stream(GD_HEADS, LANES, LANES), per_stream(SUBLANES, GD_CONV_WIDTH)],
        out_shape=[jax.ShapeDtypeStruct((rows, RW_WIDTH), BF16),
                   jax.ShapeDtypeStruct((batch, RW_HEADS, RW_HEAD_DIM, RW_HEAD_DIM), F32),
                   jax.ShapeDtypeStruct((batch, SUBLANES, RW_SHIFT_WIDTH), F32),
                   jax.ShapeDtypeStruct((rows, GD_WIDTH), BF16),
                   jax.ShapeDtypeStruct((batch, GD_HEADS, LANES, LANES), F32),
                   jax.ShapeDtypeStruct((batch, SUBLANES, GD_CONV_WIDTH), F32)],
        scratch_shapes=[pltpu.VMEM((N_PAIRS, LANES, LANES), F32)],
        compiler_params=pltpu.CompilerParams(dimension_semantics=("arbitrary", "arbitrary"),
                                             vmem_limit_bytes=VMEM_LIMIT),
    )(proj_rw, hist_rw, s_rw, *[consts[nm] for nm in rw_names],
      proj_gd, hist_gd, s_gd, *[consts[nm] for nm in gd_names])


def _out_kernel(h_ref, yrw_ref, ygd_ref, mg_ref, woa_ref, wob_ref, wo_ref, g_ref, o_ref, *, n_lead):
    @pl.when(pl.program_id(1) >= n_lead)
    def _():
        mg = mg_ref[...]
        ya = jnp.dot(yrw_ref[...], woa_ref[...], preferred_element_type=F32)
        yb = jnp.dot(ygd_ref[...], wob_ref[...], preferred_element_type=F32)
        merged = _sigmoid(mg[:, :D_MODEL]) * ya + _sigmoid(mg[:, D_MODEL:]) * yb
        out = _dot(merged, wo_ref[...])
        ms = jnp.mean(out * out, axis=-1, keepdims=True)
        o_ref[...] = h_ref[...] + out * lax.rsqrt(ms + NORM_EPS) * g_ref[...]


def _merge_out(x, lead, y_rw, y_gd, proj_mg, consts):
    batch, seq, _ = x.shape
    tm, nb, n_lead = _row_tiling(x, lead)
    x_spec = pl.BlockSpec((None, tm, D_MODEL), lambda b, i: (b, jnp.maximum(i - n_lead, 0), 0))
    rows = lambda w: pl.BlockSpec((tm, w), lambda b, i: (b * nb + i, 0))
    wspec = pl.BlockSpec((D_MODEL, D_MODEL), lambda b, i: (0, 0))
    return pl.pallas_call(
        functools.partial(_out_kernel, n_lead=n_lead),
        grid=(batch, nb),
        in_specs=[x_spec, rows(RW_WIDTH), rows(GD_WIDTH), rows(SEC_MG), wspec, wspec, wspec,
                  pl.BlockSpec((1, D_MODEL), lambda b, i: (0, 0))],
        out_specs=x_spec,
        out_shape=jax.ShapeDtypeStruct((batch, seq, D_MODEL), F32),
        compiler_params=pltpu.CompilerParams(dimension_semantics=("arbitrary", "arbitrary"),
                                             vmem_limit_bytes=VMEM_LIMIT),
    )(x, y_rw, y_gd, proj_mg, consts['w_oa'], consts['w_ob'], consts['w_o'], consts['norm_post'])


def _chunk_consts(chunk, n_sub):
    L = chunk
    tril = np.tril(np.ones((L, L), np.float32))
    ea_l = np.zeros((LANES, GD_HEADS * L), np.float32)
    for h in range(GD_HEADS):
        ea_l[GD_HEADS + h, h * L:(h + 1) * L] = 1.0
    return {'tril': jnp.asarray(np.kron(np.eye(n_sub, dtype=np.float32), tril), BF16),
            'triu': jnp.asarray(tril.T, BF16), 'e_alpha_l': jnp.asarray(ea_l, BF16)}


def _layer_consts(norm_post, w_in, rw_mu, rw_w0, rw_w2, rw_a0, rw_a2, rw_k_k, rw_k_a, rw_r_k, rw_ln_w,
                  rw_ln_b, gd_conv_w, gd_a_log, gd_dt_bias, gd_norm_w, w_out_a, w_out_b, w_out):
    off_gate = RW_SHIFT_WIDTH
    off_conv = off_gate + RW_WIDTH
    off_beta = off_conv + GD_CONV_WIDTH
    off_ggate = off_beta + 2 * GD_HEADS
    off_merge = off_ggate + GD_WIDTH
    w_bf = w_in.astype(BF16)
    pad = jnp.zeros((D_MODEL, LANES - 2 * GD_HEADS), BF16)
    w2a2 = jnp.zeros((2 * RW_LORA, 2 * RW_WIDTH), F32)
    w2a2 = w2a2.at[:RW_LORA, :RW_WIDTH].set(rw_w2).at[RW_LORA:, RW_WIDTH:].set(rw_a2)
    lane_row = lambda vec: jnp.zeros((1, LANES), F32).at[0, GD_HEADS:2 * GD_HEADS].set(vec)
    row = lambda a: a.reshape(1, -1).astype(F32)
    return {
        'w_rw': w_bf[:, :off_conv],
        'w_gd': jnp.concatenate([w_bf[:, off_conv:off_ggate], pad, w_bf[:, off_ggate:off_merge]], axis=1),
        'w_mg': w_bf[:, off_merge:],
        'mu': row(rw_mu), 'w0': row(rw_w0), 'a0': row(rw_a0), 'k_k': row(rw_k_k), 'k_a': row(rw_k_a),
        'r_k': row(rw_r_k), 'ln_w': row(rw_ln_w), 'ln_b': row(rw_ln_b),
        'w2a2': w2a2.astype(BF16),
        'conv_w': gd_conv_w.astype(F32), 'a_log': lane_row(gd_a_log), 'dt_bias': lane_row(gd_dt_bias),
        'gn_w': row(jnp.tile(gd_norm_w, GD_HEADS)),
        'w_oa': w_out_a.astype(BF16), 'w_ob': w_out_b.astype(BF16), 'w_o': w_out.astype(BF16),
        'norm_post': row(norm_post),
    }


def _trunk(x, lead, streams, chunk, n_sub, shift_prev, conv_prev, s_rw, s_gd, norm_pre, consts):
    total = x.shape[0] * (x.shape[1] + (0 if lead is None else lead.shape[0]))
    t = total // streams
    n_chunks = t // chunk
    cc = dict(consts, **_chunk_consts(chunk, n_sub))
    xn = _prenorm(x, lead, norm_pre)
    proj_rw = _proj(xn, consts['w_rw'])
    proj_gd = _proj(xn, consts['w_gd'])
    proj_mg = _proj(xn, consts['w_mg'])
    hist_rw = jnp.zeros((streams, SUBLANES, RW_SHIFT_WIDTH), F32).at[:, SUBLANES - 1].set(shift_prev)
    hist_gd = jnp.zeros((streams, SUBLANES, GD_CONV_WIDTH), F32).at[:, SUBLANES - (GD_CONV - 1):].set(conv_prev)
    y_rw, s_rw_new, last_rw, y_gd, s_gd_new, last_gd = _recurrences(
        proj_rw, hist_rw, s_rw, proj_gd, hist_gd, s_gd, cc, streams, n_chunks, chunk, n_sub)
    x_new = _merge_out(x, lead, y_rw, y_gd, proj_mg, consts)
    shift_new = last_rw[:, SUBLANES - 1]
    conv_new = last_gd[:, SUBLANES - (GD_CONV - 1):]
    return x_new, (shift_new, s_rw_new, conv_new, s_gd_new)


def kernel(x_prompt, x_sample, state_rwkv_shift, state_rwkv_wkv, state_gdn_conv, state_gdn_ssm, meta_tokens, norm_pre, w_in, rw_mu, rw_w0, rw_w2, rw_a0, rw_a2, rw_k_k, rw_k_a, rw_r_k, rw_ln_w, rw_ln_b, gd_conv_w, gd_a_log, gd_dt_bias, gd_norm_w, w_out_a, w_out_b, w_out, norm_post):
    assert w_in.shape[0] == 1, "single layer: the meta rows' outputs are not carried to a next layer"
    l = 0
    bp, seq, _ = x_prompt.shape
    bs, dec_seq, _ = x_sample.shape
    n_front = (-(N_META + seq)) % (PROMPT_CHUNK * PROMPT_SUB)
    lead = jnp.concatenate([jnp.zeros((n_front, D_MODEL), F32), meta_tokens.astype(F32)], axis=0)
    consts = _layer_consts(norm_post[l], w_in[l], rw_mu[l], rw_w0[l], rw_w2[l], rw_a0[l], rw_a2[l],
                           rw_k_k[l], rw_k_a[l], rw_r_k[l], rw_ln_w[l], rw_ln_b[l], gd_conv_w[l],
                           gd_a_log[l], gd_dt_bias[l], gd_norm_w[l], w_out_a[l], w_out_b[l], w_out[l])
    y_prompt, new_p = _trunk(x_prompt, lead, bp, PROMPT_CHUNK, PROMPT_SUB,
                             jnp.zeros((bp, RW_SHIFT_WIDTH), F32),
                             jnp.zeros((bp, GD_CONV - 1, GD_CONV_WIDTH), F32),
                             jnp.zeros((bp, RW_HEADS, RW_HEAD_DIM, RW_HEAD_DIM), F32),
                             jnp.zeros((bp, GD_HEADS, GD_HEAD_DIM, GD_HEAD_DIM), F32),
                             norm_pre[l], consts)
    ys, new_s = _trunk(x_sample.reshape(1, bs * dec_seq, D_MODEL), None, bs, dec_seq, 1,
                       state_rwkv_shift[l], state_gdn_conv[l], state_rwkv_wkv[l], state_gdn_ssm[l],
                       norm_pre[l], consts)
    y_sample = ys.reshape(bs, dec_seq, D_MODEL)
    return (y_prompt, y_sample) + tuple(t[None] for t in new_p) + tuple(t[None] for t in new_s)
```
